```python
import jax, jax.numpy as jnp
from jax import lax
import numpy as np

D_MODEL = 1024
BATCH = 2
SEQ = 16384
DEPTH = 4
DEC_BATCH = 8
DEC_SEQ = 8192
PAST_LEN = 128

GRID_W = 64
HEAD_DIM = 64
A_Q_HEADS = 8
A_KV_HEADS = 2
A_GROUPS = A_Q_HEADS // A_KV_HEADS
B_HEADS = 8
NA_ROWS_MAX = 8
NA_COLS = 16
Q_BLOCK = 128
ROPE_THETA = 10000.0
EPS = 1e-6
D_FF = -(-8 * D_MODEL // (3 * 256)) * 256
A_WIDTH = A_Q_HEADS * HEAD_DIM
A_KV_WIDTH = A_KV_HEADS * HEAD_DIM
B_WIDTH = B_HEADS * HEAD_DIM
IN_COLS = A_WIDTH + 2 * A_KV_WIDTH + 3 * B_WIDTH + 2 * D_MODEL
IN_SPLITS = (
    A_WIDTH,
    A_WIDTH + A_KV_WIDTH,
    A_WIDTH + 2 * A_KV_WIDTH,
    A_WIDTH + 2 * A_KV_WIDTH + B_WIDTH,
    A_WIDTH + 2 * A_KV_WIDTH + 2 * B_WIDTH,
    A_WIDTH + 2 * A_KV_WIDTH + 3 * B_WIDTH,
    A_WIDTH + 2 * A_KV_WIDTH + 3 * B_WIDTH + D_MODEL,
)

kernel_name = "hybrid_gqa_natten_gated_encoder"


def rmsnorm(x, g):
    xf = x.astype(jnp.float32)
    y = xf * lax.rsqrt(jnp.mean(xf * xf, axis=-1, keepdims=True) + EPS)
    return (y * g.astype(jnp.float32)).astype(x.dtype)


def axial_rope_tables(seq_len):
    t = jnp.arange(seq_len, dtype=jnp.int32)
    row = (t // GRID_W).astype(jnp.float32)
    col = (t % GRID_W).astype(jnp.float32)
    quarter = HEAD_DIM // 4
    inv = ROPE_THETA ** (-jnp.arange(quarter, dtype=jnp.float32) / quarter)
    ang_r = row[:, None, None] * inv
    ang_c = col[:, None, None] * inv
    return (jnp.cos(ang_r), jnp.sin(ang_r), jnp.cos(ang_c), jnp.sin(ang_c))


def _rotate(x, cos, sin):
    x1, x2 = jnp.split(x, 2, axis=-1)
    return jnp.concatenate([x1 * cos - x2 * sin, x2 * cos + x1 * sin], axis=-1)


def axial_rope(x, tables):
    cos_r, sin_r, cos_c, sin_c = tables
    xf = x.astype(jnp.float32)
    xr, xc = jnp.split(xf, 2, axis=-1)
    out = jnp.concatenate([_rotate(xr, cos_r, sin_r), _rotate(xc, cos_c, sin_c)], axis=-1)
    return out.astype(x.dtype)


def global_attention(q, k, v):
    b, s = q.shape[0], q.shape[1]
    nb = s // Q_BLOCK
    qb = q.reshape(b, nb, Q_BLOCK, A_KV_HEADS, A_GROUPS, HEAD_DIM).transpose(1, 0, 3, 4, 2, 5)
    kt = k.transpose(0, 2, 1, 3)
    vt = v.transpose(0, 2, 1, 3)
    scale = HEAD_DIM ** -0.5

    def block(qblk):
        sc = jnp.einsum('bkgqd,bksd->bkgqs', qblk, kt).astype(jnp.float32) * scale
        p = jax.nn.softmax(sc, axis=-1)
        return jnp.einsum('bkgqs,bksd->bkgqd', p.astype(vt.dtype), vt)

    o = lax.map(block, qb)
    return o.transpose(1, 0, 4, 2, 3, 5).reshape(b, s, A_WIDTH)


def neighbourhood_attention(q, k, v, rpb):
    b, s = q.shape[0], q.shape[1]
    rows = s // GRID_W
    kh = min(NA_ROWS_MAX, rows)

    def grid(t):
        return t.reshape(b, rows, GRID_W, B_HEADS, HEAD_DIM).transpose(0, 3, 1, 2, 4)

    qg, kg, vg = grid(q), grid(k), grid(v)
    r = jnp.arange(rows, dtype=jnp.int32)
    rs = jnp.clip(r - kh // 2, 0, rows - kh)
    row_idx = rs[:, None] + jnp.arange(kh, dtype=jnp.int32)[None, :]
    k_rows = jnp.take(kg, row_idx, axis=2)
    v_rows = jnp.take(vg, row_idx, axis=2)
    c = jnp.arange(GRID_W, dtype=jnp.int32)
    cs = jnp.clip(c - NA_COLS // 2, 0, GRID_W - NA_COLS)
    col_mask = (c[None, :] >= cs[:, None]) & (c[None, :] < cs[:, None] + NA_COLS)
    dr = row_idx - r[:, None] + (NA_ROWS_MAX - 1)
    dc = jnp.clip(c[None, :] - c[:, None], -(NA_COLS - 1), NA_COLS - 1) + (NA_COLS - 1)
    bias = rpb[:, dr][..., dc]
    bias = bias.transpose(0, 1, 3, 2, 4).astype(jnp.float32)
    scale = HEAD_DIM ** -0.5
    sc = jnp.einsum('bhrqd,bhrixd->bhrqix', qg, k_rows).astype(jnp.float32) * scale + bias
    sc = jnp.where(col_mask[:, None, :], sc, -jnp.inf)
    p = jax.nn.softmax(sc, axis=(-2, -1))
    o = jnp.einsum('bhrqix,bhrixd->bhrqd', p.astype(v_rows.dtype), v_rows)
    return o.transpose(0, 2, 3, 1, 4).reshape(b, s, B_WIDTH)


def encoder_layer(x, tables, norm1, w_in, q_norm, k_norm, rpb, w_up_a, w_up_b, w_out,
                  norm2, w_gate_up, w_down):
    b, s, _ = x.shape
    h = rmsnorm(x, norm1)
    proj = h @ w_in
    qa, ka, va, qb, kb, vb, ga, gb = jnp.split(proj, IN_SPLITS, axis=-1)
    qa = axial_rope(rmsnorm(qa.reshape(b, s, A_Q_HEADS, HEAD_DIM), q_norm), tables)
    ka = axial_rope(rmsnorm(ka.reshape(b, s, A_KV_HEADS, HEAD_DIM), k_norm), tables)
    va = va.reshape(b, s, A_KV_HEADS, HEAD_DIM)
    ya = global_attention(qa, ka, va)
    yb = neighbourhood_attention(qb.reshape(b, s, B_HEADS, HEAD_DIM),
                                 kb.reshape(b, s, B_HEADS, HEAD_DIM),
                                 vb.reshape(b, s, B_HEADS, HEAD_DIM), rpb)
    mix = jax.nn.sigmoid(ga) * (ya @ w_up_a) + jax.nn.sigmoid(gb) * (yb @ w_up_b)
    x = x + mix @ w_out
    h = rmsnorm(x, norm2)
    gate, up = jnp.split(h @ w_gate_up, 2, axis=-1)
    return x + (jax.nn.silu(gate) * up) @ w_down


def trunk(x, norm1, w_in, q_norm, k_norm, rpb, w_up_a, w_up_b, w_out, norm2, w_gate_up,
          w_down, final_norm):
    tables = axial_rope_tables(x.shape[1])
    for l in range(DEPTH):
        x = encoder_layer(x, tables, norm1[l], w_in[l], q_norm[l], k_norm[l], rpb[l],
                          w_up_a[l], w_up_b[l], w_out[l], norm2[l], w_gate_up[l], w_down[l])
    return rmsnorm(x, final_norm)


def setup_inputs(seed: int = 0) -> dict:
    key = jax.random.key(seed)
    ks = jax.random.split(key, 16)
    f32 = jnp.float32
    resid = (2 * DEPTH) ** -0.5

    def nrm(k, shape, scale):
        return jax.random.normal(k, shape, f32) * scale

    return {
        "x_prompt": nrm(ks[0], (BATCH, SEQ, D_MODEL), 1.0),
        "x_sample": nrm(ks[1], (DEC_BATCH, DEC_SEQ, D_MODEL), 1.0),
        "norm1": 1.0 + nrm(ks[2], (DEPTH, D_MODEL), 0.02),
        "w_in": nrm(ks[3], (DEPTH, D_MODEL, IN_COLS), D_MODEL ** -0.5),
        "q_norm": 1.0 + nrm(ks[4], (DEPTH, HEAD_DIM), 0.02),
        "k_norm": 1.0 + nrm(ks[5], (DEPTH, HEAD_DIM), 0.02),
        "rpb": nrm(ks[6], (DEPTH, B_HEADS, 2 * NA_ROWS_MAX - 1, 2 * NA_COLS - 1), 0.1),
        "w_up_a": nrm(ks[7], (DEPTH, A_WIDTH, D_MODEL), A_WIDTH ** -0.5),
        "w_up_b": nrm(ks[8], (DEPTH, B_WIDTH, D_MODEL), B_WIDTH ** -0.5),
        "w_out": nrm(ks[9], (DEPTH, D_MODEL, D_MODEL), D_MODEL ** -0.5 * resid),
        "norm2": 1.0 + nrm(ks[10], (DEPTH, D_MODEL), 0.02),
        "w_gate_up": nrm(ks[11], (DEPTH, D_MODEL, 2 * D_FF), D_MODEL ** -0.5),
        "w_down": nrm(ks[12], (DEPTH, D_FF, D_MODEL), D_FF ** -0.5 * resid),
        "final_norm": 1.0 + nrm(ks[13], (D_MODEL,), 0.02),
    }


def reference(x_prompt, x_sample, norm1, w_in, q_norm, k_norm, rpb, w_up_a, w_up_b, w_out,
              norm2, w_gate_up, w_down, final_norm):
    y_prompt = trunk(x_prompt, norm1, w_in, q_norm, k_norm, rpb, w_up_a, w_up_b, w_out,
                     norm2, w_gate_up, w_down, final_norm)
    y_sample = trunk(x_sample, norm1, w_in, q_norm, k_norm, rpb, w_up_a, w_up_b, w_out,
                     norm2, w_gate_up, w_down, final_norm)
    return (y_prompt, y_sample)
```

```python
import functools

import numpy as np
import jax
import jax.numpy as jnp
from jax import lax
from jax.experimental import pallas as pl
from jax.experimental.pallas import tpu as pltpu

D_MODEL = 1024
GRID_W = 64
HEAD_DIM = 64
A_Q_HEADS = 8
A_KV_HEADS = 2
A_GROUPS = A_Q_HEADS // A_KV_HEADS
B_HEADS = 8
NA_ROWS = 8
NA_COLS = 16
ROPE_THETA = 10000.0
EPS = 1e-6
D_FF = -(-8 * D_MODEL // (3 * 256)) * 256
A_WIDTH = A_Q_HEADS * HEAD_DIM
A_KV_WIDTH = A_KV_HEADS * HEAD_DIM
B_WIDTH = B_HEADS * HEAD_DIM
QK_SCALE = HEAD_DIM ** -0.5
NEG = -1e30

R_QA = 0
R_KA = R_QA + A_WIDTH
R_VA = R_KA + A_KV_WIDTH
R_QB = R_VA + A_KV_WIDTH
R_KB = R_QB + B_WIDTH
R_VB = R_KB + B_WIDTH
R_GA = R_VB + B_WIDTH
R_GB = R_GA + D_MODEL
IN_COLS = R_GB + D_MODEL

NA_PAIR = 2 * GRID_W
NA_WIN_BLOCKS = 5
NA_WIN = NA_WIN_BLOCKS * NA_PAIR
NA_CLASSES = 5

VMEM_LIMIT = 56 * 1024 * 1024

TM = 512
TQ = 512
TK = 256

f32 = jnp.float32
bf16 = jnp.bfloat16


def _params(*sem):
    return pltpu.CompilerParams(dimension_semantics=sem, vmem_limit_bytes=VMEM_LIMIT)


def _full(shape):
    return pl.BlockSpec(shape, lambda *_: (0,) * len(shape))


def _to_feature_major_kernel(x_ref, o_ref):
    o_ref[...] = x_ref[...].T


def to_feature_major(x2d):
    t = x2d.shape[0]
    return pl.pallas_call(
        _to_feature_major_kernel,
        grid=(t // TM,),
        in_specs=[pl.BlockSpec((TM, D_MODEL), lambda i: (i, 0))],
        out_specs=pl.BlockSpec((D_MODEL, TM), lambda i: (0, i)),
        out_shape=jax.ShapeDtypeStruct((D_MODEL, t), f32),
        compiler_params=_params("parallel"),
        name="to_feature_major",
    )(x2d)


def _final_norm_kernel(x_ref, g_ref, o_ref):
    x = x_ref[...]
    ms = jnp.mean(x * x, axis=0, keepdims=True)
    y = x * lax.rsqrt(ms + EPS) * g_ref[...]
    o_ref[...] = y.T


def final_norm_out(xt, g):
    t = xt.shape[1]
    return pl.pallas_call(
        _final_norm_kernel,
        grid=(t // TM,),
        in_specs=[pl.BlockSpec((D_MODEL, TM), lambda i: (0, i)), _full((D_MODEL, 1))],
        out_specs=pl.BlockSpec((TM, D_MODEL), lambda i: (i, 0)),
        out_shape=jax.ShapeDtypeStruct((t, D_MODEL), f32),
        compiler_params=_params("parallel"),
        name="final_norm",
    )(xt, g.reshape(D_MODEL, 1))


def _head_norm_rope(y, gain, cos, sin):
    nh = y.shape[0] // HEAD_DIM
    y = y.reshape(nh, HEAD_DIM, y.shape[1])
    ms = jnp.mean(y * y, axis=1, keepdims=True)
    y = y * lax.rsqrt(ms + EPS) * gain
    q = HEAD_DIM // 4
    rot = jnp.concatenate([y[:, q:2 * q], y[:, 0:q], y[:, 3 * q:4 * q], y[:, 2 * q:3 * q]], axis=1)
    y = y * cos + rot * sin
    return y.reshape(nh * HEAD_DIM, y.shape[2])


def _in_proj_kernel(x_ref, g1_ref, w_ref, qg_ref, kg_ref, cos_ref, sin_ref,
                    qa_ref, ka_ref, va_ref, qb_ref, kb_ref, vb_ref, sga_ref, sgb_ref):
    x = x_ref[...]
    ms = jnp.mean(x * x, axis=0, keepdims=True)
    h = (x * lax.rsqrt(ms + EPS) * g1_ref[...]).astype(bf16)
    cos = cos_ref[...]
    sin = sin_ref[...]

    def proj(r0, r1):
        return jnp.dot(w_ref[r0:r1, :], h, preferred_element_type=f32)

    qa = _head_norm_rope(proj(R_QA, R_KA), qg_ref[...], cos, sin)
    qa_ref[...] = (qa * QK_SCALE).astype(bf16)
    ka = _head_norm_rope(proj(R_KA, R_VA), kg_ref[...], cos, sin)
    ka_ref[...] = ka.T.astype(bf16)
    va = proj(R_VA, R_QB).astype(bf16)
    tk = va_ref.shape[2]
    for c in range(va_ref.shape[0]):
        va_ref[c] = va[:, c * tk:(c + 1) * tk]
    qb_ref[...] = (proj(R_QB, R_KB) * QK_SCALE).astype(bf16)
    kb_ref[...] = proj(R_KB, R_VB).T.astype(bf16)
    vb_ref[...] = proj(R_VB, R_GA).astype(bf16)
    sga_ref[...] = jax.nn.sigmoid(proj(R_GA, R_GB)).astype(bf16)
    sgb_ref[...] = jax.nn.sigmoid(proj(R_GB, IN_COLS)).astype(bf16)


def in_proj(xt, g1, w_in_t, qg, kg, cos, sin, seq):
    t = xt.shape[1]
    n_seq_tiles = seq // TM
    tab = pl.BlockSpec((HEAD_DIM, TM), lambda i: (0, i % n_seq_tiles))

    def fm(rows):
        return pl.BlockSpec((rows, TM), lambda i: (0, i))

    def tok(cols):
        return pl.BlockSpec((TM, cols), lambda i: (i, 0))

    out_shapes = (
        jax.ShapeDtypeStruct((A_WIDTH, t), bf16),
        jax.ShapeDtypeStruct((t, A_KV_WIDTH), bf16),
        jax.ShapeDtypeStruct((t // TK, A_KV_WIDTH, TK), bf16),
        jax.ShapeDtypeStruct((B_WIDTH, t), bf16),
        jax.ShapeDtypeStruct((t, B_WIDTH), bf16),
        jax.ShapeDtypeStruct((B_WIDTH, t), bf16),
        jax.ShapeDtypeStruct((D_MODEL, t), bf16),
        jax.ShapeDtypeStruct((D_MODEL, t), bf16),
    )
    out_specs = (
        fm(A_WIDTH), tok(A_KV_WIDTH),
        pl.BlockSpec((TM // TK, A_KV_WIDTH, TK), lambda i: (i, 0, 0)),
        fm(B_WIDTH), tok(B_WIDTH), fm(B_WIDTH), fm(D_MODEL), fm(D_MODEL),
    )
    return pl.pallas_call(
        _in_proj_kernel,
        grid=(t // TM,),
        in_specs=[fm(D_MODEL), _full((D_MODEL, 1)), _full((IN_COLS, D_MODEL)),
                  _full((HEAD_DIM, 1)), _full((HEAD_DIM, 1)), tab, tab],
        out_specs=out_specs,
        out_shape=out_shapes,
        compiler_params=_params("parallel"),
        name="in_proj",
    )(xt, g1.reshape(D_MODEL, 1), w_in_t, qg.reshape(HEAD_DIM, 1), kg.reshape(HEAD_DIM, 1), cos, sin)


def _global_attn_kernel(q_ref, k_ref, v_ref, o_ref):
    tq = q_ref.shape[1]
    n_kv = v_ref.shape[0]
    tk = v_ref.shape[2]
    zeros = jnp.zeros((HEAD_DIM, tq), bf16)
    for h in range(A_Q_HEADS):
        j = h // A_GROUPS
        q = q_ref[h * HEAD_DIM:(h + 1) * HEAD_DIM, :]
        qpad = jnp.concatenate([q, zeros] if j == 0 else [zeros, q], axis=0)

        def body(i, carry, qpad=qpad, j=j):
            m, l, acc = carry
            k = k_ref[pl.ds(pl.multiple_of(i * tk, tk), tk), :]
            s = jnp.dot(k, qpad, preferred_element_type=f32)
            m_new = jnp.maximum(m, jnp.max(s, axis=0, keepdims=True))
            alpha = jnp.exp(m - m_new)
            p = jnp.exp(s - m_new)
            l = alpha * l + jnp.sum(p, axis=0, keepdims=True)
            v = v_ref[i, j * HEAD_DIM:(j + 1) * HEAD_DIM, :]
            acc = alpha * acc + jnp.dot(v, p.astype(bf16), preferred_element_type=f32)
            return m_new, l, acc

        init = (jnp.full((1, tq), NEG, f32), jnp.zeros((1, tq), f32), jnp.zeros((HEAD_DIM, tq), f32))
        m, l, acc = lax.fori_loop(0, n_kv, body, init)
        o_ref[h * HEAD_DIM:(h + 1) * HEAD_DIM, :] = (acc / l).astype(bf16)


def global_attn(qa, ka, va, batch, seq):
    t = batch * seq
    n_q = seq // TQ
    n_kv = seq // TK
    return pl.pallas_call(
        _global_attn_kernel,
        grid=(batch, n_q),
        in_specs=[pl.BlockSpec((A_WIDTH, TQ), lambda b, i: (0, b * n_q + i)),
                  pl.BlockSpec((seq, A_KV_WIDTH), lambda b, i: (b, 0)),
                  pl.BlockSpec((n_kv, A_KV_WIDTH, TK), lambda b, i: (b, 0, 0))],
        out_specs=pl.BlockSpec((A_WIDTH, TQ), lambda b, i: (0, b * n_q + i)),
        out_shape=jax.ShapeDtypeStruct((A_WIDTH, t), bf16),
        compiler_params=_params("parallel", "parallel"),
        name="global_attn",
    )(qa, ka, va)


def _na_bias_index(rows):
    cls = [(0, (0, 0)), (2, (0, 0)), (4, (0, 1)), (6, (2, 2)), (8, (2, 2))]
    i = np.arange(NA_WIN_BLOCKS * 2)[:, None, None, None]
    x = np.arange(GRID_W)[None, :, None, None]
    u = np.arange(2)[None, None, :, None]
    c = np.arange(GRID_W)[None, None, None, :]
    cs = np.clip(c - NA_COLS // 2, 0, GRID_W - NA_COLS)
    col_ok = (x >= cs) & (x < cs + NA_COLS)
    dc = np.clip(x - c, -(NA_COLS - 1), NA_COLS - 1) + (NA_COLS - 1)
    idx, ok = [], []
    for delta, rs_off in cls:
        off = np.asarray(rs_off)[None, None, :, None]
        a = i - delta - u + (NA_ROWS - 1)
        row_ok = (i - off >= 0) & (i - off < NA_ROWS)
        valid = np.broadcast_to(row_ok & col_ok, (NA_WIN_BLOCKS * 2, GRID_W, 2, GRID_W))
        flat = np.clip(a, 0, 2 * NA_ROWS - 2) * (2 * NA_COLS - 1) + dc
        flat = np.broadcast_to(flat, valid.shape)
        idx.append(flat.reshape(NA_WIN, NA_PAIR))
        ok.append(valid.reshape(NA_WIN, NA_PAIR))
    return np.stack(idx).astype(np.int32), np.stack(ok)


def na_bias_tiles(rpb_l, rows):
    idx, ok = _na_bias_index(rows)
    flat = rpb_l.reshape(B_HEADS, -1).astype(f32)
    tiles = jnp.take(flat, jnp.asarray(idx), axis=1)
    tiles = jnp.where(jnp.asarray(ok)[None], tiles, NEG)
    return tiles.transpose(1, 0, 2, 3)


def _nbr_attn_kernel(q_ref, k0, k1, k2, k3, k4, v0, v1, v2, v3, v4, b_ref, o_ref):
    ks = (k0, k1, k2, k3, k4)
    vs = (v0, v1, v2, v3, v4)
    zeros = jnp.zeros((HEAD_DIM, NA_PAIR), bf16)
    for hp in range(B_HEADS // 2):
        kwin = jnp.concatenate([k[:, hp * 128:(hp + 1) * 128] for k in ks], axis=0)
        for e in range(2):
            h = hp * 2 + e
            rows = slice(h * HEAD_DIM, (h + 1) * HEAD_DIM)
            q = q_ref[rows, :]
            qpad = jnp.concatenate([q, zeros] if e == 0 else [zeros, q], axis=0)
            s = jnp.dot(kwin, qpad, preferred_element_type=f32) + b_ref[0, h]
            m = jnp.max(s, axis=0, keepdims=True)
            p = jnp.exp(s - m)
            l = jnp.sum(p, axis=0, keepdims=True)
            vwin = jnp.concatenate([v[rows, :] for v in vs], axis=1)
            o = jnp.dot(vwin, p.astype(bf16), preferred_element_type=f32)
            o_ref[rows, :] = (o / l).astype(bf16)


def nbr_attn(qb, kb, vb, bias, batch, seq):
    t = batch * seq
    n_pairs = seq // NA_PAIR
    last = n_pairs - NA_WIN_BLOCKS

    def win(p):
        return jnp.clip(p - 2, 0, last)

    def cls(p):
        return jnp.where(p < 2, p, jnp.where(p >= n_pairs - 2, p - (n_pairs - 2) + 3, 2))

    k_specs = [pl.BlockSpec((NA_PAIR, B_WIDTH), lambda b, p, c=c: (b * n_pairs + win(p) + c, 0))
               for c in range(NA_WIN_BLOCKS)]
    v_specs = [pl.BlockSpec((B_WIDTH, NA_PAIR), lambda b, p, c=c: (0, b * n_pairs + win(p) + c))
               for c in range(NA_WIN_BLOCKS)]
    qo_spec = pl.BlockSpec((B_WIDTH, NA_PAIR), lambda b, p: (0, b * n_pairs + p))
    b_spec = pl.BlockSpec((1, B_HEADS, NA_WIN, NA_PAIR), lambda b, p: (cls(p), 0, 0, 0))
    return pl.pallas_call(
        _nbr_attn_kernel,
        grid=(batch, n_pairs),
        in_specs=[qo_spec] + k_specs + v_specs + [b_spec],
        out_specs=qo_spec,
        out_shape=jax.ShapeDtypeStruct((B_WIDTH, t), bf16),
        compiler_params=_params("parallel", "arbitrary"),
        name="nbr_attn",
    )(qb, *([kb] * NA_WIN_BLOCKS), *([vb] * NA_WIN_BLOCKS), bias)


def _mix_out_kernel(x_ref, ya_ref, yb_ref, sga_ref, sgb_ref, wa_ref, wb_ref, wo_ref, o_ref):
    ua = jnp.dot(wa_ref[...], ya_ref[...], preferred_element_type=f32)
    ub = jnp.dot(wb_ref[...], yb_ref[...], preferred_element_type=f32)
    mix = sga_ref[...].astype(f32) * ua + sgb_ref[...].astype(f32) * ub
    o_ref[...] = x_ref[...] + jnp.dot(wo_ref[...], mix.astype(bf16), preferred_element_type=f32)


def mix_out(xt, ya, yb, sga, sgb, wa_t, wb_t, wo_t):
    t = xt.shape[1]

    def fm(rows):
        return pl.BlockSpec((rows, TM), lambda i: (0, i))

    return pl.pallas_call(
        _mix_out_kernel,
        grid=(t // TM,),
        in_specs=[fm(D_MODEL), fm(A_WIDTH), fm(B_WIDTH), fm(D_MODEL), fm(D_MODEL),
                  _full((D_MODEL, A_WIDTH)), _full((D_MODEL, B_WIDTH)), _full((D_MODEL, D_MODEL))],
        out_specs=fm(D_MODEL),
        out_shape=jax.ShapeDtypeStruct((D_MODEL, t), f32),
        compiler_params=_params("parallel"),
        name="mix_out",
    )(xt, ya, yb, sga, sgb, wa_t, wb_t, wo_t)


FF_CHUNKS = ((0, 1024), (1024, 2048), (2048, D_FF))


def _ffn_kernel(x_ref, g2_ref, wgu_ref, wd_ref, o_ref):
    x = x_ref[...]
    ms = jnp.mean(x * x, axis=0, keepdims=True)
    h = (x * lax.rsqrt(ms + EPS) * g2_ref[...]).astype(bf16)
    acc = x
    for c0, c1 in FF_CHUNKS:
        gate = jnp.dot(wgu_ref[c0:c1, :], h, preferred_element_type=f32)
        up = jnp.dot(wgu_ref[D_FF + c0:D_FF + c1, :], h, preferred_element_type=f32)
        act = (jax.nn.silu(gate) * up).astype(bf16)
        acc = acc + jnp.dot(wd_ref[:, c0:c1], act, preferred_element_type=f32)
    o_ref[...] = acc


def ffn(xt, g2, wgu_t, wd_t):
    t = xt.shape[1]
    blk = pl.BlockSpec((D_MODEL, TM), lambda i: (0, i))
    return pl.pallas_call(
        _ffn_kernel,
        grid=(t // TM,),
        in_specs=[blk, _full((D_MODEL, 1)), _full((2 * D_FF, D_MODEL)), _full((D_MODEL, D_FF))],
        out_specs=blk,
        out_shape=jax.ShapeDtypeStruct((D_MODEL, t), f32),
        compiler_params=_params("parallel"),
        name="ffn",
    )(xt, g2.reshape(D_MODEL, 1), wgu_t, wd_t)


def _rope_tables(seq):
    t = jnp.arange(seq, dtype=jnp.int32)
    row = (t // GRID_W).astype(f32)
    col = (t % GRID_W).astype(f32)
    quarter = HEAD_DIM // 4
    inv = ROPE_THETA ** (-jnp.arange(quarter, dtype=f32) / quarter)
    ang_r = inv[:, None] * row[None, :]
    ang_c = inv[:, None] * col[None, :]
    cos = jnp.concatenate([jnp.cos(ang_r)] * 2 + [jnp.cos(ang_c)] * 2, axis=0)
    sin = jnp.concatenate([-jnp.sin(ang_r), jnp.sin(ang_r), -jnp.sin(ang_c), jnp.sin(ang_c)], axis=0)
    return cos, sin


def _trunk(x, weights, final_g):
    batch, seq, _ = x.shape
    assert seq % max(TM, TQ, TK) == 0 and (seq // GRID_W) % 2 == 0 and seq // NA_PAIR >= NA_WIN_BLOCKS
    cos, sin = _rope_tables(seq)
    xt = to_feature_major(x.reshape(batch * seq, D_MODEL))
    for w in weights:
        qa, ka, va, qb, kb, vb, sga, sgb = in_proj(
            xt, w["norm1"], w["w_in_t"], w["q_norm"], w["k_norm"], cos, sin, seq)
        ya = global_attn(qa, ka, va, batch, seq)
        yb = nbr_attn(qb, kb, vb, na_bias_tiles(w["rpb"], seq // GRID_W), batch, seq)
        xt = mix_out(xt, ya, yb, sga, sgb, w["w_up_a_t"], w["w_up_b_t"], w["w_out_t"])
        xt = ffn(xt, w["norm2"], w["w_gate_up_t"], w["w_down_t"])
    return final_norm_out(xt, final_g).reshape(batch, seq, D_MODEL)


def kernel(x_prompt, x_sample, norm1, w_in, q_norm, k_norm, rpb, w_up_a, w_up_b, w_out, norm2, w_gate_up,
           w_down, final_norm):
    depth = norm1.shape[0]

    def wt(w):
        return w.T.astype(bf16)

    weights = [dict(norm1=norm1[l], w_in_t=wt(w_in[l]), q_norm=q_norm[l], k_norm=k_norm[l], rpb=rpb[l],
                    w_up_a_t=wt(w_up_a[l]), w_up_b_t=wt(w_up_b[l]), w_out_t=wt(w_out[l]), norm2=norm2[l],
                    w_gate_up_t=wt(w_gate_up[l]), w_down_t=wt(w_down[l])) for l in range(depth)]
    return (_trunk(x_prompt, weights, final_norm), _trunk(x_sample, weights, final_norm))
```

```python
import functools

import numpy as np
import jax
import jax.numpy as jnp
from jax import lax
from jax.experimental import pallas as pl
from jax.experimental.pallas import tpu as pltpu

D_MODEL = 1024
GRID_W = 64
HEAD_DIM = 64
A_Q_HEADS = 8
A_KV_HEADS = 2
A_GROUPS = A_Q_HEADS // A_KV_HEADS
B_HEADS = 8
NA_ROWS = 8
NA_COLS = 16
ROPE_THETA = 10000.0
EPS = 1e-6
D_FF = -(-8 * D_MODEL // (3 * 256)) * 256
A_WIDTH = A_Q_HEADS * HEAD_DIM
A_KV_WIDTH = A_KV_HEADS * HEAD_DIM
B_WIDTH = B_HEADS * HEAD_DIM
QK_SCALE = HEAD_DIM ** -0.5
LOG2E = 1.4426950408889634
V_ROWS = HEAD_DIM + 16
NEG = -1e30

R_QA = 0
R_KA = R_QA + A_WIDTH
R_VA = R_KA + A_KV_WIDTH
R_QB = R_VA + A_KV_WIDTH
R_KB = R_QB + B_WIDTH
R_VB = R_KB + B_WIDTH
R_GA = R_VB + B_WIDTH
R_GB = R_GA + D_MODEL
IN_COLS = R_GB + D_MODEL

NA_PAIR = 2 * GRID_W
NA_WIN_BLOCKS = 5
NA_WIN = NA_WIN_BLOCKS * NA_PAIR
NA_CLASSES = 5

VMEM_LIMIT = 56 * 1024 * 1024

TM = 512
TQ = 512
TK = 512

f32 = jnp.float32
bf16 = jnp.bfloat16


def _params(*sem):
    return pltpu.CompilerParams(dimension_semantics=sem, vmem_limit_bytes=VMEM_LIMIT)


def _full(shape):
    return pl.BlockSpec(shape, lambda *_: (0,) * len(shape))


def _to_feature_major_kernel(x_ref, o_ref):
    o_ref[...] = x_ref[...].T


def to_feature_major(x2d):
    t = x2d.shape[0]
    return pl.pallas_call(
        _to_feature_major_kernel,
        grid=(t // TM,),
        in_specs=[pl.BlockSpec((TM, D_MODEL), lambda i: (i, 0))],
        out_specs=pl.BlockSpec((D_MODEL, TM), lambda i: (0, i)),
        out_shape=jax.ShapeDtypeStruct((D_MODEL, t), f32),
        compiler_params=_params("parallel"),
        name="to_feature_major",
    )(x2d)


def _final_norm_kernel(x_ref, g_ref, o_ref):
    x = x_ref[...]
    ms = jnp.mean(x * x, axis=0, keepdims=True)
    y = x * lax.rsqrt(ms + EPS) * g_ref[...]
    o_ref[...] = y.T


def final_norm_out(xt, g):
    t = xt.shape[1]
    return pl.pallas_call(
        _final_norm_kernel,
        grid=(t // TM,),
        in_specs=[pl.BlockSpec((D_MODEL, TM), lambda i: (0, i)), _full((D_MODEL, 1))],
        out_specs=pl.BlockSpec((TM, D_MODEL), lambda i: (i, 0)),
        out_shape=jax.ShapeDtypeStruct((t, D_MODEL), f32),
        compiler_params=_params("parallel"),
        name="final_norm",
    )(xt, g.reshape(D_MODEL, 1))


def _head_norm_rope(y, gain, cos, sin):
    nh = y.shape[0] // HEAD_DIM
    y = y.reshape(nh, HEAD_DIM, y.shape[1])
    ms = jnp.mean(y * y, axis=1, keepdims=True)
    y = y * lax.rsqrt(ms + EPS) * gain
    q = HEAD_DIM // 4
    rot = jnp.concatenate([y[:, q:2 * q], y[:, 0:q], y[:, 3 * q:4 * q], y[:, 2 * q:3 * q]], axis=1)
    y = y * cos + rot * sin
    return y.reshape(nh * HEAD_DIM, y.shape[2])


def _in_proj_kernel(x_ref, g1_ref, w_ref, qg_ref, kg_ref, cos_ref, sin_ref,
                    qa_ref, ka_ref, va_ref, qb_ref, kb_ref, vb_ref, sga_ref, sgb_ref):
    x = x_ref[...]
    ms = jnp.mean(x * x, axis=0, keepdims=True)
    h = (x * lax.rsqrt(ms + EPS) * g1_ref[...]).astype(bf16)
    cos = cos_ref[...]
    sin = sin_ref[...]

    def proj(r0, r1):
        return jnp.dot(w_ref[r0:r1, :], h, preferred_element_type=f32)

    qa = _head_norm_rope(proj(R_QA, R_KA), qg_ref[...], cos, sin)
    qa_ref[...] = (qa * (QK_SCALE * LOG2E)).astype(bf16)
    ka = _head_norm_rope(proj(R_KA, R_VA), kg_ref[...], cos, sin)
    ka_ref[...] = ka.T.astype(bf16)
    va = proj(R_VA, R_QB).astype(bf16)
    tk = va_ref.shape[2]
    ones = jnp.ones((V_ROWS - HEAD_DIM, tk), bf16)
    for c in range(va_ref.shape[0]):
        cols = slice(c * tk, (c + 1) * tk)
        va_ref[c] = jnp.concatenate([va[0:HEAD_DIM, cols], ones, va[HEAD_DIM:2 * HEAD_DIM, cols], ones], axis=0)
    qb_ref[...] = (proj(R_QB, R_KB) * QK_SCALE).astype(bf16)
    kb_ref[...] = proj(R_KB, R_VB).T.astype(bf16)
    vb_ref[...] = proj(R_VB, R_GA).astype(bf16)
    sga_ref[...] = jax.nn.sigmoid(proj(R_GA, R_GB)).astype(bf16)
    sgb_ref[...] = jax.nn.sigmoid(proj(R_GB, IN_COLS)).astype(bf16)


def in_proj(xt, g1, w_in_t, qg, kg, cos, sin, seq):
    t = xt.shape[1]
    n_seq_tiles = seq // TM
    tab = pl.BlockSpec((HEAD_DIM, TM), lambda i: (0, i % n_seq_tiles))

    def fm(rows):
        return pl.BlockSpec((rows, TM), lambda i: (0, i))

    def tok(cols):
        return pl.BlockSpec((TM, cols), lambda i: (i, 0))

    out_shapes = (
        jax.ShapeDtypeStruct((A_WIDTH, t), bf16),
        jax.ShapeDtypeStruct((t, A_KV_WIDTH), bf16),
        jax.ShapeDtypeStruct((t // TK, A_KV_HEADS * V_ROWS, TK), bf16),
        jax.ShapeDtypeStruct((B_WIDTH, t), bf16),
        jax.ShapeDtypeStruct((t, B_WIDTH), bf16),
        jax.ShapeDtypeStruct((B_WIDTH, t), bf16),
        jax.ShapeDtypeStruct((D_MODEL, t), bf16),
        jax.ShapeDtypeStruct((D_MODEL, t), bf16),
    )
    out_specs = (
        fm(A_WIDTH), tok(A_KV_WIDTH),
        pl.BlockSpec((TM // TK, A_KV_HEADS * V_ROWS, TK), lambda i: (i, 0, 0)),
        fm(B_WIDTH), tok(B_WIDTH), fm(B_WIDTH), fm(D_MODEL), fm(D_MODEL),
    )
    return pl.pallas_call(
        _in_proj_kernel,
        grid=(t // TM,),
        in_specs=[fm(D_MODEL), _full((D_MODEL, 1)), _full((IN_COLS, D_MODEL)),
                  _full((HEAD_DIM, 1)), _full((HEAD_DIM, 1)), tab, tab],
        out_specs=out_specs,
        out_shape=out_shapes,
        compiler_params=_params("parallel"),
        name="in_proj",
    )(xt, g1.reshape(D_MODEL, 1), w_in_t, qg.reshape(HEAD_DIM, 1), kg.reshape(HEAD_DIM, 1), cos, sin)


def _global_attn_kernel(q_ref, k_ref, v_ref, o_ref):
    tq = q_ref.shape[1]
    n_kv = v_ref.shape[0]
    tk = v_ref.shape[2]
    n = A_GROUPS * tq
    j = pl.program_id(1)
    q = jnp.concatenate([q_ref[g * HEAD_DIM:(g + 1) * HEAD_DIM, :] for g in range(A_GROUPS)], axis=1)
    zeros = jnp.zeros_like(q)
    qpad = jnp.concatenate([jnp.where(j == 0, q, zeros), jnp.where(j == 1, q, zeros)], axis=0)

    def scores(i):
        k = k_ref[pl.ds(pl.multiple_of(i * tk, tk), tk), :]
        return jnp.dot(k, qpad, preferred_element_type=f32)

    def update(i, s, m, l, acc):
        m_new = jnp.maximum(m, jnp.max(s, axis=0, keepdims=True))
        alpha = jnp.exp2(m - m_new)
        p = jnp.exp2(s - m_new).astype(bf16)
        pv = jnp.dot(v_ref[i], p, preferred_element_type=f32)
        l = alpha * l + pv[HEAD_DIM:HEAD_DIM + 1, :]
        acc = alpha * acc + pv[0:HEAD_DIM, :]
        return m_new, l, acc

    def body(i, carry):
        return update(i, scores(i), *carry)

    init = (jnp.full((1, n), NEG, f32), jnp.zeros((1, n), f32), jnp.zeros((HEAD_DIM, n), f32))
    m, l, acc = lax.fori_loop(0, n_kv, body, init)
    out = (acc / l).astype(bf16)
    for g in range(A_GROUPS):
        o_ref[g * HEAD_DIM:(g + 1) * HEAD_DIM, :] = out[:, g * tq:(g + 1) * tq]


def global_attn(qa, ka, va, batch, seq):
    t = batch * seq
    n_q = seq // TQ
    n_kv = seq // TK
    qo_spec = pl.BlockSpec((A_GROUPS * HEAD_DIM, TQ), lambda b, j, i: (j, b * n_q + i))
    return pl.pallas_call(
        _global_attn_kernel,
        grid=(batch, A_KV_HEADS, n_q),
        in_specs=[qo_spec,
                  pl.BlockSpec((seq, A_KV_WIDTH), lambda b, j, i: (b, 0)),
                  pl.BlockSpec((n_kv, V_ROWS, TK), lambda b, j, i: (b, j, 0))],
        out_specs=qo_spec,
        out_shape=jax.ShapeDtypeStruct((A_WIDTH, t), bf16),
        compiler_params=_params("parallel", "parallel", "parallel"),
        name="global_attn",
    )(qa, ka, va)


def _na_bias_index():
    x = np.arange(GRID_W)[:, None]
    c = np.arange(GRID_W)[None, :]
    cs = np.clip(c - NA_COLS // 2, 0, GRID_W - NA_COLS)
    col_ok = (x >= cs) & (x < cs + NA_COLS)
    dc = np.clip(x - c, -(NA_COLS - 1), NA_COLS - 1) + (NA_COLS - 1)
    onehot = (dc[None] == np.arange(2 * NA_COLS - 1)[:, None, None]).astype(np.float32)
    cls = [(0, (0, 0)), (2, (0, 0)), (4, (0, 1)), (6, (2, 2)), (8, (2, 2))]
    i = np.arange(NA_WIN_BLOCKS * 2)[:, None]
    u = np.arange(2)[None, :]
    rel, row_ok = [], []
    for delta, rs_off in cls:
        off = np.asarray(rs_off)[None, :]
        rel.append(np.clip(i - delta - u + (NA_ROWS - 1), 0, 2 * NA_ROWS - 2))
        row_ok.append((i - off >= 0) & (i - off < NA_ROWS))
    return onehot, col_ok, np.stack(rel).astype(np.int32), np.stack(row_ok)


def na_bias_tiles(rpb):
    onehot, col_ok, rel, row_ok = _na_bias_index()
    t1 = jnp.einsum("lhad,dxc->lhaxc", rpb.astype(f32), jnp.asarray(onehot), precision=lax.Precision.HIGHEST)
    g = jnp.take(t1, jnp.asarray(rel.reshape(-1)), axis=2)
    g = g.reshape(rpb.shape[0], B_HEADS, NA_CLASSES, NA_WIN_BLOCKS * 2, 2, GRID_W, GRID_W)
    ok = row_ok[:, :, :, None, None] & col_ok[None, None, None]
    g = jnp.where(jnp.asarray(ok)[None, None], g, NEG)
    g = g.transpose(0, 2, 1, 3, 5, 4, 6)
    return g.reshape(rpb.shape[0], NA_CLASSES, B_HEADS, NA_WIN, NA_PAIR)


def _nbr_attn_kernel(q_ref, k0, k1, k2, k3, k4, v0, v1, v2, v3, v4, b_ref, o_ref):
    ks = (k0, k1, k2, k3, k4)
    vs = (v0, v1, v2, v3, v4)
    zeros = jnp.zeros((HEAD_DIM, NA_PAIR), bf16)
    for hp in range(B_HEADS // 2):
        kwin = jnp.concatenate([k[:, hp * 128:(hp + 1) * 128] for k in ks], axis=0)
        for e in range(2):
            h = hp * 2 + e
            rows = slice(h * HEAD_DIM, (h + 1) * HEAD_DIM)
            q = q_ref[rows, :]
            qpad = jnp.concatenate([q, zeros] if e == 0 else [zeros, q], axis=0)
            s = jnp.dot(kwin, qpad, preferred_element_type=f32) + b_ref[0, h]
            m = jnp.max(s, axis=0, keepdims=True)
            p = jnp.exp(s - m)
            l = jnp.sum(p, axis=0, keepdims=True)
            vwin = jnp.concatenate([v[rows, :] for v in vs], axis=1)
            o = jnp.dot(vwin, p.astype(bf16), preferred_element_type=f32)
            o_ref[rows, :] = (o / l).astype(bf16)


def nbr_attn(qb, kb, vb, bias, batch, seq):
    t = batch * seq
    n_pairs = seq // NA_PAIR
    last = n_pairs - NA_WIN_BLOCKS

    def win(p):
        return jnp.clip(p - 2, 0, last)

    def cls(p):
        return jnp.where(p < 2, p, jnp.where(p >= n_pairs - 2, p - (n_pairs - 2) + 3, 2))

    k_specs = [pl.BlockSpec((NA_PAIR, B_WIDTH), lambda b, p, c=c: (b * n_pairs + win(p) + c, 0))
               for c in range(NA_WIN_BLOCKS)]
    v_specs = [pl.BlockSpec((B_WIDTH, NA_PAIR), lambda b, p, c=c: (0, b * n_pairs + win(p) + c))
               for c in range(NA_WIN_BLOCKS)]
    qo_spec = pl.BlockSpec((B_WIDTH, NA_PAIR), lambda b, p: (0, b * n_pairs + p))
    b_spec = pl.BlockSpec((1, B_HEADS, NA_WIN, NA_PAIR), lambda b, p: (cls(p), 0, 0, 0))
    return pl.pallas_call(
        _nbr_attn_kernel,
        grid=(batch, n_pairs),
        in_specs=[qo_spec] + k_specs + v_specs + [b_spec],
        out_specs=qo_spec,
        out_shape=jax.ShapeDtypeStruct((B_WIDTH, t), bf16),
        compiler_params=_params("parallel", "arbitrary"),
        name="nbr_attn",
    )(qb, *([kb] * NA_WIN_BLOCKS), *([vb] * NA_WIN_BLOCKS), bias)


def _mix_out_kernel(x_ref, ya_ref, yb_ref, sga_ref, sgb_ref, wa_ref, wb_ref, wo_ref, o_ref):
    ua = jnp.dot(wa_ref[...], ya_ref[...], preferred_element_type=f32)
    ub = jnp.dot(wb_ref[...], yb_ref[...], preferred_element_type=f32)
    mix = sga_ref[...].astype(f32) * ua + sgb_ref[...].astype(f32) * ub
    o_ref[...] = x_ref[...] + jnp.dot(wo_ref[...], mix.astype(bf16), preferred_element_type=f32)


def mix_out(xt, ya, yb, sga, sgb, wa_t, wb_t, wo_t):
    t = xt.shape[1]

    def fm(rows):
        return pl.BlockSpec((rows, TM), lambda i: (0, i))

    return pl.pallas_call(
        _mix_out_kernel,
        grid=(t // TM,),
        in_specs=[fm(D_MODEL), fm(A_WIDTH), fm(B_WIDTH), fm(D_MODEL), fm(D_MODEL),
                  _full((D_MODEL, A_WIDTH)), _full((D_MODEL, B_WIDTH)), _full((D_MODEL, D_MODEL))],
        out_specs=fm(D_MODEL),
        out_shape=jax.ShapeDtypeStruct((D_MODEL, t), f32),
        compiler_params=_params("parallel"),
        name="mix_out",
    )(xt, ya, yb, sga, sgb, wa_t, wb_t, wo_t)


FF_CHUNKS = ((0, 1024), (1024, 2048), (2048, D_FF))


def _ffn_kernel(x_ref, g2_ref, wgu_ref, wd_ref, o_ref):
    x = x_ref[...]
    ms = jnp.mean(x * x, axis=0, keepdims=True)
    h = (x * lax.rsqrt(ms + EPS) * g2_ref[...]).astype(bf16)
    acc = x
    for c0, c1 in FF_CHUNKS:
        gate = jnp.dot(wgu_ref[c0:c1, :], h, preferred_element_type=f32)
        up = jnp.dot(wgu_ref[D_FF + c0:D_FF + c1, :], h, preferred_element_type=f32)
        act = (jax.nn.silu(gate) * up).astype(bf16)
        acc = acc + jnp.dot(wd_ref[:, c0:c1], act, preferred_element_type=f32)
    o_ref[...] = acc


def ffn(xt, g2, wgu_t, wd_t):
    t = xt.shape[1]
    blk = pl.BlockSpec((D_MODEL, TM), lambda i: (0, i))
    return pl.pallas_call(
        _ffn_kernel,
        grid=(t // TM,),
        in_specs=[blk, _full((D_MODEL, 1)), _full((2 * D_FF, D_MODEL)), _full((D_MODEL, D_FF))],
        out_specs=blk,
        out_shape=jax.ShapeDtypeStruct((D_MODEL, t), f32),
        compiler_params=_params("parallel"),
        name="ffn",
    )(xt, g2.reshape(D_MODEL, 1), wgu_t, wd_t)


def _rope_tables(seq):
    t = jnp.arange(seq, dtype=jnp.int32)
    row = (t // GRID_W).astype(f32)
    col = (t % GRID_W).astype(f32)
    quarter = HEAD_DIM // 4
    inv = ROPE_THETA ** (-jnp.arange(quarter, dtype=f32) / quarter)
    ang_r = inv[:, None] * row[None, :]
    ang_c = inv[:, None] * col[None, :]
    cos = jnp.concatenate([jnp.cos(ang_r)] * 2 + [jnp.cos(ang_c)] * 2, axis=0)
    sin = jnp.concatenate([-jnp.sin(ang_r), jnp.sin(ang_r), -jnp.sin(ang_c), jnp.sin(ang_c)], axis=0)
    return cos, sin


def _trunk(x, weights, final_g):
    batch, seq, _ = x.shape
    assert seq % max(TM, TQ, TK) == 0 and (seq // GRID_W) % 2 == 0 and seq // NA_PAIR >= NA_WIN_BLOCKS
    cos, sin = _rope_tables(seq)
    xt = to_feature_major(x.reshape(batch * seq, D_MODEL))
    for w in weights:
        qa, ka, va, qb, kb, vb, sga, sgb = in_proj(
            xt, w["norm1"], w["w_in_t"], w["q_norm"], w["k_norm"], cos, sin, seq)
        ya = global_attn(qa, ka, va, batch, seq)
        yb = nbr_attn(qb, kb, vb, w["na_bias"], batch, seq)
        xt = mix_out(xt, ya, yb, sga, sgb, w["w_up_a_t"], w["w_up_b_t"], w["w_out_t"])
        xt = ffn(xt, w["norm2"], w["w_gate_up_t"], w["w_down_t"])
    return final_norm_out(xt, final_g).reshape(batch, seq, D_MODEL)


def kernel(x_prompt, x_sample, norm1, w_in, q_norm, k_norm, rpb, w_up_a, w_up_b, w_out, norm2, w_gate_up,
           w_down, final_norm):
    depth = norm1.shape[0]

    def wt(w):
        return w.T.astype(bf16)

    na_bias = na_bias_tiles(rpb)
    weights = [dict(norm1=norm1[l], w_in_t=wt(w_in[l]), q_norm=q_norm[l], k_norm=k_norm[l], na_bias=na_bias[l],
                    w_up_a_t=wt(w_up_a[l]), w_up_b_t=wt(w_up_b[l]), w_out_t=wt(w_out[l]), norm2=norm2[l],
                    w_gate_up_t=wt(w_gate_up[l]), w_down_t=wt(w_down[l])) for l in range(depth)]
    return (_trunk(x_prompt, weights, final_norm), _trunk(x_sample, weights, final_norm))
```

```python
import functools

import numpy as np
import jax
import jax.numpy as jnp
from jax import lax
from jax.experimental import pallas as pl
from jax.experimental.pallas import tpu as pltpu

D_MODEL = 1024
GRID_W = 64
HEAD_DIM = 64
A_Q_HEADS = 8
A_KV_HEADS = 2
A_GROUPS = A_Q_HEADS // A_KV_HEADS
B_HEADS = 8
NA_ROWS = 8
NA_COLS = 16
ROPE_THETA = 10000.0
EPS = 1e-6
D_FF = -(-8 * D_MODEL // (3 * 256)) * 256
A_WIDTH = A_Q_HEADS * HEAD_DIM
A_KV_WIDTH = A_KV_HEADS * HEAD_DIM
B_WIDTH = B_HEADS * HEAD_DIM
QK_SCALE = HEAD_DIM ** -0.5
LOG2E = 1.4426950408889634
V_ROWS = HEAD_DIM + 16
NEG = -1e30

R_QA = 0
R_KA = R_QA + A_WIDTH
R_VA = R_KA + A_KV_WIDTH
R_QB = R_VA + A_KV_WIDTH
R_KB = R_QB + B_WIDTH
R_VB = R_KB + B_WIDTH
R_GA = R_VB + B_WIDTH
R_GB = R_GA + D_MODEL
IN_COLS = R_GB + D_MODEL

NA_PAIR = 2 * GRID_W
NA_WIN_BLOCKS = 5
NA_WIN = NA_WIN_BLOCKS * NA_PAIR
NA_CLASSES = 5

VMEM_LIMIT = 56 * 1024 * 1024

TM = 512
TQ = 512
TK = 512
CB = 256

f32 = jnp.float32
bf16 = jnp.bfloat16


def _params(*sem):
    return pltpu.CompilerParams(dimension_semantics=sem, vmem_limit_bytes=VMEM_LIMIT)


def _full(shape):
    return pl.BlockSpec(shape, lambda *_: (0,) * len(shape))


def _to_feature_major_kernel(x_ref, o_ref):
    o_ref[...] = x_ref[...].T


def to_feature_major(x2d):
    t = x2d.shape[0]
    return pl.pallas_call(
        _to_feature_major_kernel,
        grid=(t // TM,),
        in_specs=[pl.BlockSpec((TM, D_MODEL), lambda i: (i, 0))],
        out_specs=pl.BlockSpec((D_MODEL, TM), lambda i: (0, i)),
        out_shape=jax.ShapeDtypeStruct((D_MODEL, t), f32),
        compiler_params=_params("parallel"),
        name="to_feature_major",
    )(x2d)


def _final_norm_kernel(x_ref, g_ref, o_ref):
    x = x_ref[...]
    ms = jnp.mean(x * x, axis=0, keepdims=True)
    y = x * lax.rsqrt(ms + EPS) * g_ref[...]
    o_ref[...] = y.T


def final_norm_out(xt, g):
    t = xt.shape[1]
    return pl.pallas_call(
        _final_norm_kernel,
        grid=(t // TM,),
        in_specs=[pl.BlockSpec((D_MODEL, TM), lambda i: (0, i)), _full((D_MODEL, 1))],
        out_specs=pl.BlockSpec((TM, D_MODEL), lambda i: (i, 0)),
        out_shape=jax.ShapeDtypeStruct((t, D_MODEL), f32),
        compiler_params=_params("parallel"),
        name="final_norm",
    )(xt, g.reshape(D_MODEL, 1))


def _head_norm_rope(y, gain, cos, sin):
    nh = y.shape[0] // HEAD_DIM
    y = y.reshape(nh, HEAD_DIM, y.shape[1])
    ms = jnp.mean(y * y, axis=1, keepdims=True)
    y = y * lax.rsqrt(ms + EPS) * gain
    q = HEAD_DIM // 4
    rot = jnp.concatenate([y[:, q:2 * q], y[:, 0:q], y[:, 3 * q:4 * q], y[:, 2 * q:3 * q]], axis=1)
    y = y * cos + rot * sin
    return y.reshape(nh * HEAD_DIM, y.shape[2])


def _in_proj_kernel(x_ref, g1_ref, w_ref, qg_ref, kg_ref, cos_ref, sin_ref,
                    qa_ref, ka_ref, va_ref, qb_ref, kb_ref, vb_ref, sga_ref, sgb_ref):
    x = x_ref[...]
    ms = jnp.mean(x * x, axis=0, keepdims=True)
    h = (x * lax.rsqrt(ms + EPS) * g1_ref[...]).astype(bf16)
    cos = cos_ref[...]
    sin = sin_ref[...]

    def proj(r0, r1):
        return jnp.dot(w_ref[r0:r1, :], h, preferred_element_type=f32)

    qa = _head_norm_rope(proj(R_QA, R_KA), qg_ref[...], cos, sin)
    qa_ref[...] = (qa * (QK_SCALE * LOG2E)).astype(bf16)
    ka = _head_norm_rope(proj(R_KA, R_VA), kg_ref[...], cos, sin)
    ka_ref[...] = ka.T.astype(bf16)
    va = proj(R_VA, R_QB).astype(bf16)
    tk = va_ref.shape[2]
    ones = jnp.ones((V_ROWS - HEAD_DIM, tk), bf16)
    for c in range(va_ref.shape[0]):
        cols = slice(c * tk, (c + 1) * tk)
        va_ref[c] = jnp.concatenate([va[0:HEAD_DIM, cols], ones, va[HEAD_DIM:2 * HEAD_DIM, cols], ones], axis=0)
    qb_ref[...] = (proj(R_QB, R_KB) * (QK_SCALE * LOG2E)).astype(bf16)
    kb_ref[...] = proj(R_KB, R_VB).T.astype(bf16)
    vb_ref[...] = proj(R_VB, R_GA).astype(bf16)
    sga_ref[...] = jax.nn.sigmoid(proj(R_GA, R_GB)).astype(bf16)
    sgb_ref[...] = jax.nn.sigmoid(proj(R_GB, IN_COLS)).astype(bf16)


def in_proj(xt, g1, w_in_t, qg, kg, cos, sin, seq):
    t = xt.shape[1]
    n_seq_tiles = seq // TM
    tab = pl.BlockSpec((HEAD_DIM, TM), lambda i: (0, i % n_seq_tiles))

    def fm(rows):
        return pl.BlockSpec((rows, TM), lambda i: (0, i))

    def tok(cols):
        return pl.BlockSpec((TM, cols), lambda i: (i, 0))

    out_shapes = (
        jax.ShapeDtypeStruct((A_WIDTH, t), bf16),
        jax.ShapeDtypeStruct((t, A_KV_WIDTH), bf16),
        jax.ShapeDtypeStruct((t // TK, A_KV_HEADS * V_ROWS, TK), bf16),
        jax.ShapeDtypeStruct((B_WIDTH, t), bf16),
        jax.ShapeDtypeStruct((t, B_WIDTH), bf16),
        jax.ShapeDtypeStruct((B_WIDTH, t), bf16),
        jax.ShapeDtypeStruct((D_MODEL, t), bf16),
        jax.ShapeDtypeStruct((D_MODEL, t), bf16),
    )
    out_specs = (
        fm(A_WIDTH), tok(A_KV_WIDTH),
        pl.BlockSpec((TM // TK, A_KV_HEADS * V_ROWS, TK), lambda i: (i, 0, 0)),
        fm(B_WIDTH), tok(B_WIDTH), fm(B_WIDTH), fm(D_MODEL), fm(D_MODEL),
    )
    return pl.pallas_call(
        _in_proj_kernel,
        grid=(t // TM,),
        in_specs=[fm(D_MODEL), _full((D_MODEL, 1)), _full((IN_COLS, D_MODEL)),
                  _full((HEAD_DIM, 1)), _full((HEAD_DIM, 1)), tab, tab],
        out_specs=out_specs,
        out_shape=out_shapes,
        compiler_params=_params("parallel"),
        name="in_proj",
    )(xt, g1.reshape(D_MODEL, 1), w_in_t, qg.reshape(HEAD_DIM, 1), kg.reshape(HEAD_DIM, 1), cos, sin)


def _global_attn_kernel(q_ref, k_ref, v_ref, o_ref, s_scr):
    tq = q_ref.shape[1]
    n_kv = v_ref.shape[0]
    tk = v_ref.shape[2]
    n = A_GROUPS * tq
    j = pl.program_id(1)
    q = jnp.concatenate([q_ref[g * HEAD_DIM:(g + 1) * HEAD_DIM, :] for g in range(A_GROUPS)], axis=1)
    zeros = jnp.zeros_like(q)
    qpad = jnp.concatenate([jnp.where(j == 0, q, zeros), jnp.where(j == 1, q, zeros)], axis=0)

    ncb = n // CB
    cols = [slice(c * CB, (c + 1) * CB) for c in range(ncb)]
    qblk = [qpad[:, c] for c in cols]

    def produce(i, slot, c):
        k = k_ref[pl.ds(pl.multiple_of(i * tk, tk), tk), :]
        s = jnp.dot(k, qblk[c], preferred_element_type=f32)
        s_scr[slot, :, cols[c]] = s
        return jnp.max(s, axis=0, keepdims=True)

    def consume(i, slot, c, mt, m, l, acc):
        m_new = jnp.maximum(m, mt)
        alpha = jnp.exp2(m - m_new)
        p = jnp.exp2(s_scr[slot, :, cols[c]] - m_new).astype(bf16)
        pv = jnp.dot(v_ref[i], p, preferred_element_type=f32)
        l = alpha * l + pv[HEAD_DIM:HEAD_DIM + 1, :]
        acc = alpha * acc + pv[0:HEAD_DIM, :]
        return m_new, l, acc

    def half(i_cur, slot_cur, i_next, slot_next, state):
        out = []
        for c in range(ncb):
            mt, m, l, acc = state[c]
            mt_next = produce(i_next, slot_next, c)
            out.append((mt_next,) + consume(i_cur, slot_cur, c, mt, m, l, acc))
        return out

    def body(it, state):
        i = 2 * it
        state = half(i, 0, i + 1, 1, state)
        return half(i + 1, 1, jnp.minimum(i + 2, n_kv - 1), 0, state)

    init = [(produce(0, 0, c), jnp.full((1, CB), NEG, f32), jnp.zeros((1, CB), f32), jnp.zeros((HEAD_DIM, CB), f32))
            for c in range(ncb)]
    state = lax.fori_loop(0, n_kv // 2, body, init)
    out = jnp.concatenate([(acc / l).astype(bf16) for _, _, l, acc in state], axis=1)
    for g in range(A_GROUPS):
        o_ref[g * HEAD_DIM:(g + 1) * HEAD_DIM, :] = out[:, g * tq:(g + 1) * tq]


def global_attn(qa, ka, va, batch, seq):
    t = batch * seq
    n_q = seq // TQ
    n_kv = seq // TK
    qo_spec = pl.BlockSpec((A_GROUPS * HEAD_DIM, TQ), lambda b, j, i: (j, b * n_q + i))
    return pl.pallas_call(
        _global_attn_kernel,
        grid=(batch, A_KV_HEADS, n_q),
        in_specs=[qo_spec,
                  pl.BlockSpec((seq, A_KV_WIDTH), lambda b, j, i: (b, 0)),
                  pl.BlockSpec((n_kv, V_ROWS, TK), lambda b, j, i: (b, j, 0))],
        out_specs=qo_spec,
        out_shape=jax.ShapeDtypeStruct((A_WIDTH, t), bf16),
        scratch_shapes=[pltpu.VMEM((2, TK, A_GROUPS * TQ), f32)],
        compiler_params=_params("parallel", "parallel", "parallel"),
        name="global_attn",
    )(qa, ka, va)


def _na_bias_index():
    x = np.arange(GRID_W)[:, None]
    c = np.arange(GRID_W)[None, :]
    cs = np.clip(c - NA_COLS // 2, 0, GRID_W - NA_COLS)
    col_ok = (x >= cs) & (x < cs + NA_COLS)
    dc = np.clip(x - c, -(NA_COLS - 1), NA_COLS - 1) + (NA_COLS - 1)
    onehot = (dc[None] == np.arange(2 * NA_COLS - 1)[:, None, None]).astype(np.float32)
    cls = [(0, (0, 0)), (2, (0, 0)), (4, (0, 1)), (6, (2, 2)), (8, (2, 2))]
    i = np.arange(NA_WIN_BLOCKS * 2)[:, None]
    u = np.arange(2)[None, :]
    rel, row_ok = [], []
    for delta, rs_off in cls:
        off = np.asarray(rs_off)[None, :]
        rel.append(np.clip(i - delta - u + (NA_ROWS - 1), 0, 2 * NA_ROWS - 2))
        row_ok.append((i - off >= 0) & (i - off < NA_ROWS))
    return onehot, col_ok, np.stack(rel).astype(np.int32), np.stack(row_ok)


def na_bias_tiles(rpb):
    onehot, col_ok, rel, row_ok = _na_bias_index()
    t1 = jnp.einsum("lhad,dxc->lhaxc", rpb.astype(f32), jnp.asarray(onehot), precision=lax.Precision.HIGHEST)
    g = jnp.take(t1, jnp.asarray(rel.reshape(-1)), axis=2)
    g = g.reshape(rpb.shape[0], B_HEADS, NA_CLASSES, NA_WIN_BLOCKS * 2, 2, GRID_W, GRID_W)
    ok = row_ok[:, :, :, None, None] & col_ok[None, None, None]
    g = jnp.where(jnp.asarray(ok)[None, None], g, NEG)
    g = g.reshape(rpb.shape[0], B_HEADS // 2, 2, NA_CLASSES, NA_WIN_BLOCKS * 2, 2, GRID_W, GRID_W)
    g = g.transpose(0, 3, 1, 4, 6, 2, 5, 7)
    return g.reshape(rpb.shape[0], NA_CLASSES, B_HEADS // 2, NA_WIN, 2 * NA_PAIR) * LOG2E


def _nbr_attn_kernel(q_ref, k0, k1, k2, k3, k4, v0, v1, v2, v3, v4, b_ref, o_ref):
    ks = (k0, k1, k2, k3, k4)
    vs = (v0, v1, v2, v3, v4)
    zeros = jnp.zeros((HEAD_DIM, NA_PAIR), bf16)
    ones = jnp.ones((16, NA_WIN), bf16)
    n_hp = B_HEADS // 2
    rows = [slice(hp * 2 * HEAD_DIM, (hp + 1) * 2 * HEAD_DIM) for hp in range(n_hp)]
    s = []
    for hp in range(n_hp):
        kwin = jnp.concatenate([k[:, rows[hp]] for k in ks], axis=0)
        q2 = q_ref[rows[hp], :]
        qbd = jnp.concatenate([jnp.concatenate([q2[0:HEAD_DIM], zeros], axis=1),
                               jnp.concatenate([zeros, q2[HEAD_DIM:]], axis=1)], axis=0)
        s.append(jnp.dot(kwin, qbd, preferred_element_type=f32) + b_ref[0, hp])
    m = [jnp.max(x, axis=0, keepdims=True) for x in s]
    p = [jnp.exp2(x - mx).astype(bf16) for x, mx in zip(s, m)]
    pv = []
    for hp in range(n_hp):
        vwin = jnp.concatenate([jnp.concatenate([v[rows[hp], :] for v in vs], axis=1), ones], axis=0)
        pv.append(jnp.dot(vwin, p[hp], preferred_element_type=f32))
    for hp in range(n_hp):
        o = pv[hp][0:2 * HEAD_DIM] / pv[hp][2 * HEAD_DIM:2 * HEAD_DIM + 1]
        o_ref[hp * 2 * HEAD_DIM:hp * 2 * HEAD_DIM + HEAD_DIM, :] = o[0:HEAD_DIM, 0:NA_PAIR].astype(bf16)
        o_ref[hp * 2 * HEAD_DIM + HEAD_DIM:(hp + 1) * 2 * HEAD_DIM, :] = o[HEAD_DIM:, NA_PAIR:].astype(bf16)


def nbr_attn(qb, kb, vb, bias, batch, seq):
    t = batch * seq
    n_pairs = seq // NA_PAIR
    last = n_pairs - NA_WIN_BLOCKS

    def win(p):
        return jnp.clip(p - 2, 0, last)

    def cls(p):
        return jnp.where(p < 2, p, jnp.where(p >= n_pairs - 2, p - (n_pairs - 2) + 3, 2))

    k_specs = [pl.BlockSpec((NA_PAIR, B_WIDTH), lambda b, p, c=c: (b * n_pairs + win(p) + c, 0))
               for c in range(NA_WIN_BLOCKS)]
    v_specs = [pl.BlockSpec((B_WIDTH, NA_PAIR), lambda b, p, c=c: (0, b * n_pairs + win(p) + c))
               for c in range(NA_WIN_BLOCKS)]
    qo_spec = pl.BlockSpec((B_WIDTH, NA_PAIR), lambda b, p: (0, b * n_pairs + p))
    b_spec = pl.BlockSpec((1, B_HEADS // 2, NA_WIN, 2 * NA_PAIR), lambda b, p: (cls(p), 0, 0, 0))
    return pl.pallas_call(
        _nbr_attn_kernel,
        grid=(batch, n_pairs),
        in_specs=[qo_spec] + k_specs + v_specs + [b_spec],
        out_specs=qo_spec,
        out_shape=jax.ShapeDtypeStruct((B_WIDTH, t), bf16),
        compiler_params=_params("parallel", "arbitrary"),
        name="nbr_attn",
    )(qb, *([kb] * NA_WIN_BLOCKS), *([vb] * NA_WIN_BLOCKS), bias)


def _mix_out_kernel(x_ref, ya_ref, yb_ref, sga_ref, sgb_ref, wa_ref, wb_ref, wo_ref, o_ref):
    ua = jnp.dot(wa_ref[...], ya_ref[...], preferred_element_type=f32)
    ub = jnp.dot(wb_ref[...], yb_ref[...], preferred_element_type=f32)
    mix = sga_ref[...].astype(f32) * ua + sgb_ref[...].astype(f32) * ub
    o_ref[...] = x_ref[...] + jnp.dot(wo_ref[...], mix.astype(bf16), preferred_element_type=f32)


def mix_out(xt, ya, yb, sga, sgb, wa_t, wb_t, wo_t):
    t = xt.shape[1]

    def fm(rows):
        return pl.BlockSpec((rows, TM), lambda i: (0, i))

    return pl.pallas_call(
        _mix_out_kernel,
        grid=(t // TM,),
        in_specs=[fm(D_MODEL), fm(A_WIDTH), fm(B_WIDTH), fm(D_MODEL), fm(D_MODEL),
                  _full((D_MODEL, A_WIDTH)), _full((D_MODEL, B_WIDTH)), _full((D_MODEL, D_MODEL))],
        out_specs=fm(D_MODEL),
        out_shape=jax.ShapeDtypeStruct((D_MODEL, t), f32),
        compiler_params=_params("parallel"),
        name="mix_out",
    )(xt, ya, yb, sga, sgb, wa_t, wb_t, wo_t)


FF_CHUNKS = ((0, 1024), (1024, 2048), (2048, D_FF))


def _ffn_kernel(x_ref, g2_ref, wgu_ref, wd_ref, o_ref):
    x = x_ref[...]
    ms = jnp.mean(x * x, axis=0, keepdims=True)
    h = (x * lax.rsqrt(ms + EPS) * g2_ref[...]).astype(bf16)
    acc = x
    for c0, c1 in FF_CHUNKS:
        gate = jnp.dot(wgu_ref[c0:c1, :], h, preferred_element_type=f32)
        up = jnp.dot(wgu_ref[D_FF + c0:D_FF + c1, :], h, preferred_element_type=f32)
        act = (jax.nn.silu(gate) * up).astype(bf16)
        acc = acc + jnp.dot(wd_ref[:, c0:c1], act, preferred_element_type=f32)
    o_ref[...] = acc


def ffn(xt, g2, wgu_t, wd_t):
    t = xt.shape[1]
    blk = pl.BlockSpec((D_MODEL, TM), lambda i: (0, i))
    return pl.pallas_call(
        _ffn_kernel,
        grid=(t // TM,),
        in_specs=[blk, _full((D_MODEL, 1)), _full((2 * D_FF, D_MODEL)), _full((D_MODEL, D_FF))],
        out_specs=blk,
        out_shape=jax.ShapeDtypeStruct((D_MODEL, t), f32),
        compiler_params=_params("parallel"),
        name="ffn",
    )(xt, g2.reshape(D_MODEL, 1), wgu_t, wd_t)


def _rope_tables(seq):
    t = jnp.arange(seq, dtype=jnp.int32)
    row = (t // GRID_W).astype(f32)
    col = (t % GRID_W).astype(f32)
    quarter = HEAD_DIM // 4
    inv = ROPE_THETA ** (-jnp.arange(quarter, dtype=f32) / quarter)
    ang_r = inv[:, None] * row[None, :]
    ang_c = inv[:, None] * col[None, :]
    cos = jnp.concatenate([jnp.cos(ang_r)] * 2 + [jnp.cos(ang_c)] * 2, axis=0)
    sin = jnp.concatenate([-jnp.sin(ang_r), jnp.sin(ang_r), -jnp.sin(ang_c), jnp.sin(ang_c)], axis=0)
    return cos, sin


def _trunk(x, weights, final_g):
    batch, seq, _ = x.shape
    assert seq % max(TM, TQ, TK) == 0 and (seq // GRID_W) % 2 == 0 and seq // NA_PAIR >= NA_WIN_BLOCKS
    cos, sin = _rope_tables(seq)
    xt = to_feature_major(x.reshape(batch * seq, D_MODEL))
    for w in weights:
        qa, ka, va, qb, kb, vb, sga, sgb = in_proj(
            xt, w["norm1"], w["w_in_t"], w["q_norm"], w["k_norm"], cos, sin, seq)
        ya = global_attn(qa, ka, va, batch, seq)
        yb = nbr_attn(qb, kb, vb, w["na_bias"], batch, seq)
        xt = mix_out(xt, ya, yb, sga, sgb, w["w_up_a_t"], w["w_up_b_t"], w["w_out_t"])
        xt = ffn(xt, w["norm2"], w["w_gate_up_t"], w["w_down_t"])
    return final_norm_out(xt, final_g).reshape(batch, seq, D_MODEL)


def kernel(x_prompt, x_sample, norm1, w_in, q_norm, k_norm, rpb, w_up_a, w_up_b, w_out, norm2, w_gate_up,
           w_down, final_norm):
    depth = norm1.shape[0]

    def wt(w):
        return w.T.astype(bf16)

    na_bias = na_bias_tiles(rpb)
    weights = [dict(norm1=norm1[l], w_in_t=wt(w_in[l]), q_norm=q_norm[l], k_norm=k_norm[l], na_bias=na_bias[l],
                    w_up_a_t=wt(w_up_a[l]), w_up_b_t=wt(w_up_b[l]), w_out_t=wt(w_out[l]), norm2=norm2[l],
                    w_gate_up_t=wt(w_gate_up[l]), w_down_t=wt(w_down[l])) for l in range(depth)]
    return (_trunk(x_prompt, weights, final_norm), _trunk(x_sample, weights, final_norm))
```

```python
import functools

import numpy as np
import jax
import jax.numpy as jnp
from jax import lax
from jax.experimental import pallas as pl
from jax.experimental.pallas import tpu as pltpu

D_MODEL = 1024
GRID_W = 64
HEAD_DIM = 64
A_Q_HEADS = 8
A_KV_HEADS = 2
A_GROUPS = A_Q_HEADS // A_KV_HEADS
B_HEADS = 8
NA_ROWS = 8
NA_COLS = 16
ROPE_THETA = 10000.0
EPS = 1e-6
D_FF = -(-8 * D_MODEL // (3 * 256)) * 256
A_WIDTH = A_Q_HEADS * HEAD_DIM
A_KV_WIDTH = A_KV_HEADS * HEAD_DIM
B_WIDTH = B_HEADS * HEAD_DIM
QK_SCALE = HEAD_DIM ** -0.5
LOG2E = 1.4426950408889634
V_ROWS = HEAD_DIM + 16
NEG = -1e30

R_QA = 0
R_KA = R_QA + A_WIDTH
R_VA = R_KA + A_KV_WIDTH
R_QB = R_VA + A_KV_WIDTH
R_KB = R_QB + B_WIDTH
R_VB = R_KB + B_WIDTH
R_GA = R_VB + B_WIDTH
R_GB = R_GA + D_MODEL
IN_COLS = R_GB + D_MODEL

NA_PAIR = 2 * GRID_W
NA_WIN_BLOCKS = 5
NA_WIN = NA_WIN_BLOCKS * NA_PAIR
NA_CLASSES = 5

VMEM_LIMIT = 56 * 1024 * 1024

TM = 512
TQ = 512
TK = 1024
CB = 256

f32 = jnp.float32
bf16 = jnp.bfloat16


def _params(*sem):
    return pltpu.CompilerParams(dimension_semantics=sem, vmem_limit_bytes=VMEM_LIMIT)


def _full(shape):
    return pl.BlockSpec(shape, lambda *_: (0,) * len(shape))


def _to_feature_major_kernel(x_ref, o_ref):
    o_ref[...] = x_ref[...].T


def to_feature_major(x2d):
    t = x2d.shape[0]
    return pl.pallas_call(
        _to_feature_major_kernel,
        grid=(t // TM,),
        in_specs=[pl.BlockSpec((TM, D_MODEL), lambda i: (i, 0))],
        out_specs=pl.BlockSpec((D_MODEL, TM), lambda i: (0, i)),
        out_shape=jax.ShapeDtypeStruct((D_MODEL, t), f32),
        compiler_params=_params("parallel"),
        name="to_feature_major",
    )(x2d)


def _final_norm_kernel(x_ref, g_ref, o_ref):
    x = x_ref[...]
    ms = jnp.mean(x * x, axis=0, keepdims=True)
    y = x * lax.rsqrt(ms + EPS) * g_ref[...]
    o_ref[...] = y.T


def final_norm_out(xt, g):
    t = xt.shape[1]
    return pl.pallas_call(
        _final_norm_kernel,
        grid=(t // TM,),
        in_specs=[pl.BlockSpec((D_MODEL, TM), lambda i: (0, i)), _full((D_MODEL, 1))],
        out_specs=pl.BlockSpec((TM, D_MODEL), lambda i: (i, 0)),
        out_shape=jax.ShapeDtypeStruct((t, D_MODEL), f32),
        compiler_params=_params("parallel"),
        name="final_norm",
    )(xt, g.reshape(D_MODEL, 1))


def _head_norm_rope(y, gain, cos, sin):
    nh = y.shape[0] // HEAD_DIM
    y = y.reshape(nh, HEAD_DIM, y.shape[1])
    ms = jnp.mean(y * y, axis=1, keepdims=True)
    y = y * lax.rsqrt(ms + EPS) * gain
    q = HEAD_DIM // 4
    rot = jnp.concatenate([y[:, q:2 * q], y[:, 0:q], y[:, 3 * q:4 * q], y[:, 2 * q:3 * q]], axis=1)
    y = y * cos + rot * sin
    return y.reshape(nh * HEAD_DIM, y.shape[2])


def _in_proj_kernel(x_ref, g1_ref, w_ref, qg_ref, kg_ref, cos_ref, sin_ref,
                    qa_ref, ka_ref, va_ref, qb_ref, kb_ref, vb_ref, sga_ref, sgb_ref):
    x = x_ref[...]
    ms = jnp.mean(x * x, axis=0, keepdims=True)
    h = (x * lax.rsqrt(ms + EPS) * g1_ref[...]).astype(bf16)
    cos = cos_ref[...]
    sin = sin_ref[...]

    def proj(r0, r1):
        return jnp.dot(w_ref[r0:r1, :], h, preferred_element_type=f32)

    qa = _head_norm_rope(proj(R_QA, R_KA), qg_ref[...], cos, sin)
    qa_ref[...] = (qa * (QK_SCALE * LOG2E)).astype(bf16)
    ka = _head_norm_rope(proj(R_KA, R_VA), kg_ref[...], cos, sin)
    ka_ref[...] = ka.T.astype(bf16)
    va = proj(R_VA, R_QB).astype(bf16)
    ones = jnp.ones((V_ROWS - HEAD_DIM, va.shape[1]), bf16)
    va_ref[0] = jnp.concatenate([va[0:HEAD_DIM], ones, va[HEAD_DIM:2 * HEAD_DIM], ones], axis=0)
    qb_ref[...] = (proj(R_QB, R_KB) * (QK_SCALE * LOG2E)).astype(bf16)
    kb_ref[...] = proj(R_KB, R_VB).T.astype(bf16)
    vb_ref[...] = proj(R_VB, R_GA).astype(bf16)
    sga_ref[...] = jax.nn.sigmoid(proj(R_GA, R_GB)).astype(bf16)
    sgb_ref[...] = jax.nn.sigmoid(proj(R_GB, IN_COLS)).astype(bf16)


def in_proj(xt, g1, w_in_t, qg, kg, cos, sin, seq):
    t = xt.shape[1]
    n_seq_tiles = seq // TM
    tab = pl.BlockSpec((HEAD_DIM, TM), lambda i: (0, i % n_seq_tiles))

    def fm(rows):
        return pl.BlockSpec((rows, TM), lambda i: (0, i))

    def tok(cols):
        return pl.BlockSpec((TM, cols), lambda i: (i, 0))

    out_shapes = (
        jax.ShapeDtypeStruct((A_WIDTH, t), bf16),
        jax.ShapeDtypeStruct((t, A_KV_WIDTH), bf16),
        jax.ShapeDtypeStruct((t // TK, A_KV_HEADS * V_ROWS, TK), bf16),
        jax.ShapeDtypeStruct((B_WIDTH, t), bf16),
        jax.ShapeDtypeStruct((t, B_WIDTH), bf16),
        jax.ShapeDtypeStruct((B_WIDTH, t), bf16),
        jax.ShapeDtypeStruct((D_MODEL, t), bf16),
        jax.ShapeDtypeStruct((D_MODEL, t), bf16),
    )
    out_specs = (
        fm(A_WIDTH), tok(A_KV_WIDTH),
        pl.BlockSpec((1, A_KV_HEADS * V_ROWS, TM), lambda i: (i // (TK // TM), 0, i % (TK // TM))),
        fm(B_WIDTH), tok(B_WIDTH), fm(B_WIDTH), fm(D_MODEL), fm(D_MODEL),
    )
    return pl.pallas_call(
        _in_proj_kernel,
        grid=(t // TM,),
        in_specs=[fm(D_MODEL), _full((D_MODEL, 1)), _full((IN_COLS, D_MODEL)),
                  _full((HEAD_DIM, 1)), _full((HEAD_DIM, 1)), tab, tab],
        out_specs=out_specs,
        out_shape=out_shapes,
        compiler_params=_params("parallel"),
        name="in_proj",
    )(xt, g1.reshape(D_MODEL, 1), w_in_t, qg.reshape(HEAD_DIM, 1), kg.reshape(HEAD_DIM, 1), cos, sin)


def _global_attn_kernel(q_ref, k_ref, v_ref, o_ref, s_scr):
    tq = q_ref.shape[1]
    n_kv = v_ref.shape[0]
    tk = v_ref.shape[2]
    n = A_GROUPS * tq
    j = pl.program_id(1)
    q = jnp.concatenate([q_ref[g * HEAD_DIM:(g + 1) * HEAD_DIM, :] for g in range(A_GROUPS)], axis=1)
    zeros = jnp.zeros_like(q)
    qpad = jnp.concatenate([jnp.where(j == 0, q, zeros), jnp.where(j == 1, q, zeros)], axis=0)

    ncb = n // CB
    cols = [slice(c * CB, (c + 1) * CB) for c in range(ncb)]
    qblk = [qpad[:, c] for c in cols]

    def produce(i, slot, c):
        k = k_ref[pl.ds(pl.multiple_of(i * tk, tk), tk), :]
        s = jnp.dot(k, qblk[c], preferred_element_type=f32)
        s_scr[slot, :, cols[c]] = s
        return jnp.max(s, axis=0, keepdims=True)

    def consume(i, slot, c, mt, m, l, acc):
        m_new = jnp.maximum(m, mt)
        alpha = jnp.exp2(m - m_new)
        p = jnp.exp2(s_scr[slot, :, cols[c]] - m_new).astype(bf16)
        pv = jnp.dot(v_ref[i], p, preferred_element_type=f32)
        l = alpha * l + pv[HEAD_DIM:HEAD_DIM + 1, :]
        acc = alpha * acc + pv[0:HEAD_DIM, :]
        return m_new, l, acc

    def half(i_cur, slot_cur, state, i_next=None):
        out = []
        for c in range(ncb):
            mt, m, l, acc = state[c]
            mt_next = produce(i_next, 1 - slot_cur, c) if i_next is not None else mt
            out.append((mt_next,) + consume(i_cur, slot_cur, c, mt, m, l, acc))
        return out

    def body(it, state):
        i = 2 * it
        return half(i + 1, 1, half(i, 0, state, i + 1), i + 2)

    init = [(produce(0, 0, c), jnp.full((1, CB), NEG, f32), jnp.zeros((1, CB), f32), jnp.zeros((HEAD_DIM, CB), f32))
            for c in range(ncb)]
    state = lax.fori_loop(0, n_kv // 2 - 1, body, init)
    state = half(n_kv - 1, 1, half(n_kv - 2, 0, state, n_kv - 1))
    out = jnp.concatenate([(acc / l).astype(bf16) for _, _, l, acc in state], axis=1)
    for g in range(A_GROUPS):
        o_ref[g * HEAD_DIM:(g + 1) * HEAD_DIM, :] = out[:, g * tq:(g + 1) * tq]


def global_attn(qa, ka, va, batch, seq):
    t = batch * seq
    n_q = seq // TQ
    n_kv = seq // TK
    qo_spec = pl.BlockSpec((A_GROUPS * HEAD_DIM, TQ), lambda b, j, i: (j, b * n_q + i))
    return pl.pallas_call(
        _global_attn_kernel,
        grid=(batch, A_KV_HEADS, n_q),
        in_specs=[qo_spec,
                  pl.BlockSpec((seq, A_KV_WIDTH), lambda b, j, i: (b, 0)),
                  pl.BlockSpec((n_kv, V_ROWS, TK), lambda b, j, i: (b, j, 0))],
        out_specs=qo_spec,
        out_shape=jax.ShapeDtypeStruct((A_WIDTH, t), bf16),
        scratch_shapes=[pltpu.VMEM((2, TK, A_GROUPS * TQ), f32)],
        compiler_params=_params("parallel", "parallel", "parallel"),
        name="global_attn",
    )(qa, ka, va)


def _na_bias_index():
    x = np.arange(GRID_W)[:, None]
    c = np.arange(GRID_W)[None, :]
    cs = np.clip(c - NA_COLS // 2, 0, GRID_W - NA_COLS)
    col_ok = (x >= cs) & (x < cs + NA_COLS)
    dc = np.clip(x - c, -(NA_COLS - 1), NA_COLS - 1) + (NA_COLS - 1)
    onehot = (dc[None] == np.arange(2 * NA_COLS - 1)[:, None, None]).astype(np.float32)
    cls = [(0, (0, 0)), (2, (0, 0)), (4, (0, 1)), (6, (2, 2)), (8, (2, 2))]
    i = np.arange(NA_WIN_BLOCKS * 2)[:, None]
    u = np.arange(2)[None, :]
    rel, row_ok = [], []
    for delta, rs_off in cls:
        off = np.asarray(rs_off)[None, :]
        rel.append(np.clip(i - delta - u + (NA_ROWS - 1), 0, 2 * NA_ROWS - 2))
        row_ok.append((i - off >= 0) & (i - off < NA_ROWS))
    return onehot, col_ok, np.stack(rel).astype(np.int32), np.stack(row_ok)


def na_bias_tiles(rpb):
    onehot, col_ok, rel, row_ok = _na_bias_index()
    t1 = jnp.einsum("lhad,dxc->lhaxc", rpb.astype(f32), jnp.asarray(onehot), precision=lax.Precision.HIGHEST)
    g = jnp.take(t1, jnp.asarray(rel.reshape(-1)), axis=2)
    g = g.reshape(rpb.shape[0], B_HEADS, NA_CLASSES, NA_WIN_BLOCKS * 2, 2, GRID_W, GRID_W)
    ok = row_ok[:, :, :, None, None] & col_ok[None, None, None]
    g = jnp.where(jnp.asarray(ok)[None, None], g, NEG)
    g = g.reshape(rpb.shape[0], B_HEADS // 2, 2, NA_CLASSES, NA_WIN_BLOCKS * 2, 2, GRID_W, GRID_W)
    g = g.transpose(0, 3, 1, 4, 6, 2, 5, 7)
    return g.reshape(rpb.shape[0], NA_CLASSES, B_HEADS // 2, NA_WIN, 2 * NA_PAIR) * LOG2E


def _nbr_attn_kernel(q_ref, k_ref, v_ref, b_ref, o_ref, s_scr):
    zeros = jnp.zeros((HEAD_DIM, NA_PAIR), bf16)
    ones = jnp.ones((16, NA_WIN), bf16)
    n_hp = B_HEADS // 2
    rows = [slice(hp * 2 * HEAD_DIM, (hp + 1) * 2 * HEAD_DIM) for hp in range(n_hp)]

    def produce(hp):
        kwin = k_ref[:, rows[hp]]
        q2 = q_ref[rows[hp], :]
        qbd = jnp.concatenate([jnp.concatenate([q2[0:HEAD_DIM], zeros], axis=1),
                               jnp.concatenate([zeros, q2[HEAD_DIM:]], axis=1)], axis=0)
        x = jnp.dot(kwin, qbd, preferred_element_type=f32) + b_ref[0, hp]
        s_scr[hp] = x
        return jnp.max(x, axis=0, keepdims=True)

    def consume(hp, m):
        p = jnp.exp2(s_scr[hp] - m).astype(bf16)
        vwin = jnp.concatenate([v_ref[rows[hp], :], ones], axis=0)
        pv = jnp.dot(vwin, p, preferred_element_type=f32)
        o = pv[0:2 * HEAD_DIM] / pv[2 * HEAD_DIM:2 * HEAD_DIM + 1]
        o_ref[hp * 2 * HEAD_DIM:hp * 2 * HEAD_DIM + HEAD_DIM, :] = o[0:HEAD_DIM, 0:NA_PAIR].astype(bf16)
        o_ref[hp * 2 * HEAD_DIM + HEAD_DIM:(hp + 1) * 2 * HEAD_DIM, :] = o[HEAD_DIM:, NA_PAIR:].astype(bf16)

    m = produce(0)
    for hp in range(n_hp):
        m_next = produce(hp + 1) if hp + 1 < n_hp else None
        consume(hp, m)
        m = m_next


def nbr_attn(qb, kb, vb, bias, batch, seq):
    t = batch * seq
    n_pairs = seq // NA_PAIR
    last = n_pairs - NA_WIN_BLOCKS

    def win(p):
        return jnp.clip(p - 2, 0, last)

    def cls(p):
        return jnp.where(p < 2, p, jnp.where(p >= n_pairs - 2, p - (n_pairs - 2) + 3, 2))

    k_spec = pl.BlockSpec((pl.Element(NA_WIN), pl.Element(B_WIDTH)), lambda b, p: ((b * n_pairs + win(p)) * NA_PAIR, 0))
    v_spec = pl.BlockSpec((pl.Element(B_WIDTH), pl.Element(NA_WIN)), lambda b, p: (0, (b * n_pairs + win(p)) * NA_PAIR))
    qo_spec = pl.BlockSpec((B_WIDTH, NA_PAIR), lambda b, p: (0, b * n_pairs + p))
    b_spec = pl.BlockSpec((1, B_HEADS // 2, NA_WIN, 2 * NA_PAIR), lambda b, p: (cls(p), 0, 0, 0))
    return pl.pallas_call(
        _nbr_attn_kernel,
        grid=(batch, n_pairs),
        in_specs=[qo_spec, k_spec, v_spec, b_spec],
        out_specs=qo_spec,
        out_shape=jax.ShapeDtypeStruct((B_WIDTH, t), bf16),
        scratch_shapes=[pltpu.VMEM((B_HEADS // 2, NA_WIN, 2 * NA_PAIR), f32)],
        compiler_params=_params("parallel", "arbitrary"),
        name="nbr_attn",
    )(qb, kb, vb, bias)


def _mix_out_kernel(x_ref, ya_ref, yb_ref, sga_ref, sgb_ref, wa_ref, wb_ref, wo_ref, o_ref):
    ua = jnp.dot(wa_ref[...], ya_ref[...], preferred_element_type=f32)
    ub = jnp.dot(wb_ref[...], yb_ref[...], preferred_element_type=f32)
    mix = sga_ref[...].astype(f32) * ua + sgb_ref[...].astype(f32) * ub
    o_ref[...] = x_ref[...] + jnp.dot(wo_ref[...], mix.astype(bf16), preferred_element_type=f32)


def mix_out(xt, ya, yb, sga, sgb, wa_t, wb_t, wo_t):
    t = xt.shape[1]

    def fm(rows):
        return pl.BlockSpec((rows, TM), lambda i: (0, i))

    return pl.pallas_call(
        _mix_out_kernel,
        grid=(t // TM,),
        in_specs=[fm(D_MODEL), fm(A_WIDTH), fm(B_WIDTH), fm(D_MODEL), fm(D_MODEL),
                  _full((D_MODEL, A_WIDTH)), _full((D_MODEL, B_WIDTH)), _full((D_MODEL, D_MODEL))],
        out_specs=fm(D_MODEL),
        out_shape=jax.ShapeDtypeStruct((D_MODEL, t), f32),
        compiler_params=_params("parallel"),
        name="mix_out",
    )(xt, ya, yb, sga, sgb, wa_t, wb_t, wo_t)


FF_CHUNKS = ((0, 1024), (1024, 2048), (2048, D_FF))


def _ffn_kernel(x_ref, g2_ref, wgu_ref, wd_ref, o_ref):
    x = x_ref[...]
    ms = jnp.mean(x * x, axis=0, keepdims=True)
    h = (x * lax.rsqrt(ms + EPS) * g2_ref[...]).astype(bf16)
    acc = x
    for c0, c1 in FF_CHUNKS:
        gate = jnp.dot(wgu_ref[c0:c1, :], h, preferred_element_type=f32)
        up = jnp.dot(wgu_ref[D_FF + c0:D_FF + c1, :], h, preferred_element_type=f32)
        act = (jax.nn.silu(gate) * up).astype(bf16)
        acc = acc + jnp.dot(wd_ref[:, c0:c1], act, preferred_element_type=f32)
    o_ref[...] = acc


def ffn(xt, g2, wgu_t, wd_t):
    t = xt.shape[1]
    blk = pl.BlockSpec((D_MODEL, TM), lambda i: (0, i))
    return pl.pallas_call(
        _ffn_kernel,
        grid=(t // TM,),
        in_specs=[blk, _full((D_MODEL, 1)), _full((2 * D_FF, D_MODEL)), _full((D_MODEL, D_FF))],
        out_specs=blk,
        out_shape=jax.ShapeDtypeStruct((D_MODEL, t), f32),
        compiler_params=_params("parallel"),
        name="ffn",
    )(xt, g2.reshape(D_MODEL, 1), wgu_t, wd_t)


def _rope_tables(seq):
    t = jnp.arange(seq, dtype=jnp.int32)
    row = (t // GRID_W).astype(f32)
    col = (t % GRID_W).astype(f32)
    quarter = HEAD_DIM // 4
    inv = ROPE_THETA ** (-jnp.arange(quarter, dtype=f32) / quarter)
    ang_r = inv[:, None] * row[None, :]
    ang_c = inv[:, None] * col[None, :]
    cos = jnp.concatenate([jnp.cos(ang_r)] * 2 + [jnp.cos(ang_c)] * 2, axis=0)
    sin = jnp.concatenate([-jnp.sin(ang_r), jnp.sin(ang_r), -jnp.sin(ang_c), jnp.sin(ang_c)], axis=0)
    return cos, sin


def _trunk(x, weights, final_g):
    batch, seq, _ = x.shape
    assert seq % max(TM, TQ, TK) == 0 and (seq // GRID_W) % 2 == 0 and seq // NA_PAIR >= NA_WIN_BLOCKS
    cos, sin = _rope_tables(seq)
    xt = to_feature_major(x.reshape(batch * seq, D_MODEL))
    for w in weights:
        qa, ka, va, qb, kb, vb, sga, sgb = in_proj(
            xt, w["norm1"], w["w_in_t"], w["q_norm"], w["k_norm"], cos, sin, seq)
        ya = global_attn(qa, ka, va, batch, seq)
        yb = nbr_attn(qb, kb, vb, w["na_bias"], batch, seq)
        xt = mix_out(xt, ya, yb, sga, sgb, w["w_up_a_t"], w["w_up_b_t"], w["w_out_t"])
        xt = ffn(xt, w["norm2"], w["w_gate_up_t"], w["w_down_t"])
    return final_norm_out(xt, final_g).reshape(batch, seq, D_MODEL)


def kernel(x_prompt, x_sample, norm1, w_in, q_norm, k_norm, rpb, w_up_a, w_up_b, w_out, norm2, w_gate_up,
           w_down, final_norm):
    depth = norm1.shape[0]

    def wt(w):
        return w.T.astype(bf16)

    na_bias = na_bias_tiles(rpb)
    weights = [dict(norm1=norm1[l], w_in_t=wt(w_in[l]), q_norm=q_norm[l], k_norm=k_norm[l], na_bias=na_bias[l],
                    w_up_a_t=wt(w_up_a[l]), w_up_b_t=wt(w_up_b[l]), w_out_t=wt(w_out[l]), norm2=norm2[l],
                    w_gate_up_t=wt(w_gate_up[l]), w_down_t=wt(w_down[l])) for l in range(depth)]
    return (_trunk(x_prompt, weights, final_norm), _trunk(x_sample, weights, final_norm))
```

```python
import functools

import numpy as np
import jax
import jax.numpy as jnp
from jax import lax
from jax.experimental import pallas as pl
from jax.experimental.pallas import tpu as pltpu

D_MODEL = 1024
GRID_W = 64
HEAD_DIM = 64
A_Q_HEADS = 8
A_KV_HEADS = 2
A_GROUPS = A_Q_HEADS // A_KV_HEADS
B_HEADS = 8
NA_ROWS = 8
NA_COLS = 16
ROPE_THETA = 10000.0
EPS = 1e-6
D_FF = -(-8 * D_MODEL // (3 * 256)) * 256
A_WIDTH = A_Q_HEADS * HEAD_DIM
A_KV_WIDTH = A_KV_HEADS * HEAD_DIM
B_WIDTH = B_HEADS * HEAD_DIM
QK_SCALE = HEAD_DIM ** -0.5
LOG2E = 1.4426950408889634
V_ROWS = HEAD_DIM + 16
NEG = -1e30

R_QA = 0
R_KA = R_QA + A_WIDTH
R_VA = R_KA + A_KV_WIDTH
R_QB = R_VA + A_KV_WIDTH
R_KB = R_QB + B_WIDTH
R_VB = R_KB + B_WIDTH
R_GA = R_VB + B_WIDTH
R_GB = R_GA + D_MODEL
IN_COLS = R_GB + D_MODEL

NA_PAIR = 2 * GRID_W
NA_WIN_BLOCKS = 5
NA_WIN = NA_WIN_BLOCKS * NA_PAIR
NA_CLASSES = 5

VMEM_LIMIT = 56 * 1024 * 1024

TM = 512
TQ = 512
TK = 1024
CB = 256

f32 = jnp.float32
bf16 = jnp.bfloat16


def _params(*sem):
    return pltpu.CompilerParams(dimension_semantics=sem, vmem_limit_bytes=VMEM_LIMIT)


def _full(shape):
    return pl.BlockSpec(shape, lambda *_: (0,) * len(shape))


def _to_feature_major_kernel(x_ref, o_ref):
    o_ref[...] = x_ref[...].T


def to_feature_major(x2d):
    t = x2d.shape[0]
    return pl.pallas_call(
        _to_feature_major_kernel,
        grid=(t // TM,),
        in_specs=[pl.BlockSpec((TM, D_MODEL), lambda i: (i, 0))],
        out_specs=pl.BlockSpec((D_MODEL, TM), lambda i: (0, i)),
        out_shape=jax.ShapeDtypeStruct((D_MODEL, t), f32),
        compiler_params=_params("parallel"),
        name="to_feature_major",
    )(x2d)


def _head_norm_rope(y, gain, cos, sin):
    nh = y.shape[0] // HEAD_DIM
    y = y.reshape(nh, HEAD_DIM, y.shape[1])
    ms = jnp.mean(y * y, axis=1, keepdims=True)
    y = y * lax.rsqrt(ms + EPS) * gain
    q = HEAD_DIM // 4
    rot = jnp.concatenate([y[:, q:2 * q], y[:, 0:q], y[:, 3 * q:4 * q], y[:, 2 * q:3 * q]], axis=1)
    y = y * cos + rot * sin
    return y.reshape(nh * HEAD_DIM, y.shape[2])


def _in_proj_kernel(x_ref, g1_ref, w_ref, qg_ref, kg_ref, cos_ref, sin_ref,
                    qa_ref, ka_ref, va_ref, qb_ref, kb_ref, vb_ref, sga_ref, sgb_ref):
    x = x_ref[...]
    ms = jnp.mean(x * x, axis=0, keepdims=True)
    h = (x * lax.rsqrt(ms + EPS) * g1_ref[...]).astype(bf16)
    cos = cos_ref[...]
    sin = sin_ref[...]

    def proj(r0, r1):
        return jnp.dot(w_ref[r0:r1, :], h, preferred_element_type=f32)

    qa = _head_norm_rope(proj(R_QA, R_KA), qg_ref[...], cos, sin)
    qa_ref[...] = (qa * (QK_SCALE * LOG2E)).astype(bf16)
    ka = _head_norm_rope(proj(R_KA, R_VA), kg_ref[...], cos, sin)
    ka_ref[...] = ka.T.astype(bf16)
    va = proj(R_VA, R_QB).astype(bf16)
    ones = jnp.ones((V_ROWS - HEAD_DIM, va.shape[1]), bf16)
    va_ref[0] = jnp.concatenate([va[0:HEAD_DIM], ones, va[HEAD_DIM:2 * HEAD_DIM], ones], axis=0)
    qb_ref[...] = (proj(R_QB, R_KB) * (QK_SCALE * LOG2E)).astype(bf16)
    kb_ref[...] = proj(R_KB, R_VB).T.astype(bf16)
    vb_ref[...] = proj(R_VB, R_GA).astype(bf16)
    sga_ref[...] = jax.nn.sigmoid(proj(R_GA, R_GB)).astype(bf16)
    sgb_ref[...] = jax.nn.sigmoid(proj(R_GB, IN_COLS)).astype(bf16)


def in_proj(xt, g1, w_in_t, qg, kg, cos, sin, seq):
    t = xt.shape[1]
    n_seq_tiles = seq // TM
    tab = pl.BlockSpec((HEAD_DIM, TM), lambda i: (0, i % n_seq_tiles))

    def fm(rows):
        return pl.BlockSpec((rows, TM), lambda i: (0, i))

    def tok(cols):
        return pl.BlockSpec((TM, cols), lambda i: (i, 0))

    out_shapes = (
        jax.ShapeDtypeStruct((A_WIDTH, t), bf16),
        jax.ShapeDtypeStruct((t, A_KV_WIDTH), bf16),
        jax.ShapeDtypeStruct((t // TK, A_KV_HEADS * V_ROWS, TK), bf16),
        jax.ShapeDtypeStruct((B_WIDTH, t), bf16),
        jax.ShapeDtypeStruct((t, B_WIDTH), bf16),
        jax.ShapeDtypeStruct((B_WIDTH, t), bf16),
        jax.ShapeDtypeStruct((D_MODEL, t), bf16),
        jax.ShapeDtypeStruct((D_MODEL, t), bf16),
    )
    out_specs = (
        fm(A_WIDTH), tok(A_KV_WIDTH),
        pl.BlockSpec((1, A_KV_HEADS * V_ROWS, TM), lambda i: (i // (TK // TM), 0, i % (TK // TM))),
        fm(B_WIDTH), tok(B_WIDTH), fm(B_WIDTH), fm(D_MODEL), fm(D_MODEL),
    )
    return pl.pallas_call(
        _in_proj_kernel,
        grid=(t // TM,),
        in_specs=[fm(D_MODEL), _full((D_MODEL, 1)), _full((IN_COLS, D_MODEL)),
                  _full((HEAD_DIM, 1)), _full((HEAD_DIM, 1)), tab, tab],
        out_specs=out_specs,
        out_shape=out_shapes,
        compiler_params=_params("parallel"),
        name="in_proj",
    )(xt, g1.reshape(D_MODEL, 1), w_in_t, qg.reshape(HEAD_DIM, 1), kg.reshape(HEAD_DIM, 1), cos, sin)


def _global_attn_kernel(q_ref, k_ref, v_ref, o_ref, s_scr):
    tq = q_ref.shape[1]
    n_kv = v_ref.shape[0]
    tk = v_ref.shape[2]
    n = A_GROUPS * tq
    j = pl.program_id(1)
    q = jnp.concatenate([q_ref[g * HEAD_DIM:(g + 1) * HEAD_DIM, :] for g in range(A_GROUPS)], axis=1)
    zeros = jnp.zeros_like(q)
    qpad = jnp.concatenate([jnp.where(j == 0, q, zeros), jnp.where(j == 1, q, zeros)], axis=0)

    ncb = n // CB
    cols = [slice(c * CB, (c + 1) * CB) for c in range(ncb)]
    qblk = [qpad[:, c] for c in cols]

    def produce(i, slot, c):
        k = k_ref[pl.ds(pl.multiple_of(i * tk, tk), tk), :]
        s = jnp.dot(k, qblk[c], preferred_element_type=f32)
        s_scr[slot, :, cols[c]] = s
        return jnp.max(s, axis=0, keepdims=True)

    def consume(i, slot, c, mt, m, l, acc):
        m_new = jnp.maximum(m, mt)
        alpha = jnp.exp2(m - m_new)
        p = jnp.exp2(s_scr[slot, :, cols[c]] - m_new).astype(bf16)
        pv = jnp.dot(v_ref[i], p, preferred_element_type=f32)
        l = alpha * l + pv[HEAD_DIM:HEAD_DIM + 1, :]
        acc = alpha * acc + pv[0:HEAD_DIM, :]
        return m_new, l, acc

    def half(i_cur, slot_cur, state, i_next=None):
        out = []
        for c in range(ncb):
            mt, m, l, acc = state[c]
            mt_next = produce(i_next, 1 - slot_cur, c) if i_next is not None else mt
            out.append((mt_next,) + consume(i_cur, slot_cur, c, mt, m, l, acc))
        return out

    def body(it, state):
        i = 2 * it
        return half(i + 1, 1, half(i, 0, state, i + 1), i + 2)

    init = [(produce(0, 0, c), jnp.full((1, CB), NEG, f32), jnp.zeros((1, CB), f32), jnp.zeros((HEAD_DIM, CB), f32))
            for c in range(ncb)]
    state = lax.fori_loop(0, n_kv // 2 - 1, body, init)
    state = half(n_kv - 1, 1, half(n_kv - 2, 0, state, n_kv - 1))
    out = jnp.concatenate([(acc / l).astype(bf16) for _, _, l, acc in state], axis=1)
    for g in range(A_GROUPS):
        o_ref[g * HEAD_DIM:(g + 1) * HEAD_DIM, :] = out[:, g * tq:(g + 1) * tq]


def global_attn(qa, ka, va, batch, seq):
    t = batch * seq
    n_q = seq // TQ
    n_kv = seq // TK
    qo_spec = pl.BlockSpec((A_GROUPS * HEAD_DIM, TQ), lambda b, j, i: (j, b * n_q + i))
    return pl.pallas_call(
        _global_attn_kernel,
        grid=(batch, A_KV_HEADS, n_q),
        in_specs=[qo_spec,
                  pl.BlockSpec((seq, A_KV_WIDTH), lambda b, j, i: (b, 0)),
                  pl.BlockSpec((n_kv, V_ROWS, TK), lambda b, j, i: (b, j, 0))],
        out_specs=qo_spec,
        out_shape=jax.ShapeDtypeStruct((A_WIDTH, t), bf16),
        scratch_shapes=[pltpu.VMEM((2, TK, A_GROUPS * TQ), f32)],
        compiler_params=_params("parallel", "parallel", "parallel"),
        name="global_attn",
    )(qa, ka, va)


def _na_bias_index():
    x = np.arange(GRID_W)[:, None]
    c = np.arange(GRID_W)[None, :]
    cs = np.clip(c - NA_COLS // 2, 0, GRID_W - NA_COLS)
    col_ok = (x >= cs) & (x < cs + NA_COLS)
    dc = np.clip(x - c, -(NA_COLS - 1), NA_COLS - 1) + (NA_COLS - 1)
    onehot = (dc[None] == np.arange(2 * NA_COLS - 1)[:, None, None]).astype(np.float32)
    cls = [(0, (0, 0)), (2, (0, 0)), (4, (0, 1)), (6, (2, 2)), (8, (2, 2))]
    i = np.arange(NA_WIN_BLOCKS * 2)[:, None]
    u = np.arange(2)[None, :]
    rel, row_ok = [], []
    for delta, rs_off in cls:
        off = np.asarray(rs_off)[None, :]
        rel.append(np.clip(i - delta - u + (NA_ROWS - 1), 0, 2 * NA_ROWS - 2))
        row_ok.append((i - off >= 0) & (i - off < NA_ROWS))
    return onehot, col_ok, np.stack(rel).astype(np.int32), np.stack(row_ok)


def na_bias_tiles(rpb):
    onehot, col_ok, rel, row_ok = _na_bias_index()
    t1 = jnp.einsum("lhad,dxc->lhaxc", rpb.astype(f32), jnp.asarray(onehot), precision=lax.Precision.HIGHEST)
    g = jnp.take(t1, jnp.asarray(rel.reshape(-1)), axis=2)
    g = g.reshape(rpb.shape[0], B_HEADS, NA_CLASSES, NA_WIN_BLOCKS * 2, 2, GRID_W, GRID_W)
    ok = row_ok[:, :, :, None, None] & col_ok[None, None, None]
    g = jnp.where(jnp.asarray(ok)[None, None], g, NEG)
    g = g.reshape(rpb.shape[0], B_HEADS // 2, 2, NA_CLASSES, NA_WIN_BLOCKS * 2, 2, GRID_W, GRID_W)
    g = g.transpose(0, 3, 1, 4, 6, 2, 5, 7)
    return g.reshape(rpb.shape[0], NA_CLASSES, B_HEADS // 2, NA_WIN, 2 * NA_PAIR) * LOG2E


NA_STEP_PAIRS = 2


def _nbr_attn_kernel(q_ref, *refs):
    k_refs = refs[0:NA_STEP_PAIRS]
    v_refs = refs[NA_STEP_PAIRS:2 * NA_STEP_PAIRS]
    b_refs = refs[2 * NA_STEP_PAIRS:3 * NA_STEP_PAIRS]
    o_ref, s_scr, p_scr = refs[3 * NA_STEP_PAIRS:]
    zeros = jnp.zeros((HEAD_DIM, NA_PAIR), bf16)
    ones = jnp.ones((16, NA_WIN), bf16)
    n_hp = B_HEADS // 2
    units = [(e, hp) for e in range(NA_STEP_PAIRS) for hp in range(n_hp)]

    def produce(u):
        e, hp = units[u]
        rows = slice(hp * 2 * HEAD_DIM, (hp + 1) * 2 * HEAD_DIM)
        kwin = k_refs[e][:, rows]
        q2 = q_ref[rows, e * NA_PAIR:(e + 1) * NA_PAIR]
        qbd = jnp.concatenate([jnp.concatenate([q2[0:HEAD_DIM], zeros], axis=1),
                               jnp.concatenate([zeros, q2[HEAD_DIM:]], axis=1)], axis=0)
        x = jnp.dot(kwin, qbd, preferred_element_type=f32) + b_refs[e][0, hp]
        s_scr[u % 2] = x
        return jnp.max(x, axis=0, keepdims=True)

    def softmax(u, m):
        p_scr[u % 2] = jnp.exp2(s_scr[u % 2] - m).astype(bf16)

    def weighted(u):
        e, hp = units[u]
        rows = slice(hp * 2 * HEAD_DIM, (hp + 1) * 2 * HEAD_DIM)
        vwin = jnp.concatenate([v_refs[e][rows, :], ones], axis=0)
        pv = jnp.dot(vwin, p_scr[u % 2], preferred_element_type=f32)
        o = pv[0:2 * HEAD_DIM] / pv[2 * HEAD_DIM:2 * HEAD_DIM + 1]
        cols = slice(e * NA_PAIR, (e + 1) * NA_PAIR)
        o_ref[hp * 2 * HEAD_DIM:hp * 2 * HEAD_DIM + HEAD_DIM, cols] = o[0:HEAD_DIM, 0:NA_PAIR].astype(bf16)
        o_ref[hp * 2 * HEAD_DIM + HEAD_DIM:(hp + 1) * 2 * HEAD_DIM, cols] = o[HEAD_DIM:, NA_PAIR:].astype(bf16)

    n_u = len(units)
    m = {0: produce(0)}
    m[1] = produce(1)
    softmax(0, m[0])
    for u in range(n_u):
        if u + 2 < n_u:
            m[u + 2] = produce(u + 2)
        if u + 1 < n_u:
            softmax(u + 1, m[u + 1])
        weighted(u)


def nbr_attn(qb, kb, vb, bias, batch, seq):
    t = batch * seq
    n_pairs = seq // NA_PAIR
    n_steps = n_pairs // NA_STEP_PAIRS
    last = n_pairs - NA_WIN_BLOCKS

    def win(p):
        return jnp.clip(p - 2, 0, last)

    def cls(p):
        return jnp.where(p < 2, p, jnp.where(p >= n_pairs - 2, p - (n_pairs - 2) + 3, 2))

    def k_spec(e):
        return pl.BlockSpec((pl.Element(NA_WIN), pl.Element(B_WIDTH)),
                            lambda b, g: ((b * n_pairs + win(NA_STEP_PAIRS * g + e)) * NA_PAIR, 0))

    def v_spec(e):
        return pl.BlockSpec((pl.Element(B_WIDTH), pl.Element(NA_WIN)),
                            lambda b, g: (0, (b * n_pairs + win(NA_STEP_PAIRS * g + e)) * NA_PAIR))

    def b_spec(e):
        return pl.BlockSpec((1, B_HEADS // 2, NA_WIN, 2 * NA_PAIR), lambda b, g: (cls(NA_STEP_PAIRS * g + e), 0, 0, 0))

    qo_spec = pl.BlockSpec((B_WIDTH, NA_STEP_PAIRS * NA_PAIR), lambda b, g: (0, b * n_steps + g))
    es = range(NA_STEP_PAIRS)
    return pl.pallas_call(
        _nbr_attn_kernel,
        grid=(batch, n_steps),
        in_specs=[qo_spec] + [k_spec(e) for e in es] + [v_spec(e) for e in es] + [b_spec(e) for e in es],
        out_specs=qo_spec,
        out_shape=jax.ShapeDtypeStruct((B_WIDTH, t), bf16),
        scratch_shapes=[pltpu.VMEM((2, NA_WIN, 2 * NA_PAIR), f32), pltpu.VMEM((2, NA_WIN, 2 * NA_PAIR), bf16)],
        compiler_params=_params("parallel", "arbitrary"),
        name="nbr_attn",
    )(qb, *([kb] * NA_STEP_PAIRS), *([vb] * NA_STEP_PAIRS), *([bias] * NA_STEP_PAIRS))


def _resident(shape):
    return pl.BlockSpec(shape, lambda *_: (0,) * len(shape), pipeline_mode=pl.Buffered(1))


FF_CHUNKS = ((0, 1024), (1024, 2048), (2048, D_FF))


def _mix_ffn_kernel(x_ref, ya_ref, yb_ref, sga_ref, sgb_ref, wa_ref, wb_ref, wo_ref, g2_ref, wgu_ref, wd_ref, *rest):
    o_ref = rest[-1]
    ua = jnp.dot(wa_ref[...], ya_ref[...], preferred_element_type=f32)
    ub = jnp.dot(wb_ref[...], yb_ref[...], preferred_element_type=f32)
    mix = sga_ref[...].astype(f32) * ua + sgb_ref[...].astype(f32) * ub
    x = x_ref[...] + jnp.dot(wo_ref[...], mix.astype(bf16), preferred_element_type=f32)
    ms = jnp.mean(x * x, axis=0, keepdims=True)
    h = (x * lax.rsqrt(ms + EPS) * g2_ref[...]).astype(bf16)
    acc = x
    for c0, c1 in FF_CHUNKS:
        gate = jnp.dot(wgu_ref[c0:c1, :], h, preferred_element_type=f32)
        up = jnp.dot(wgu_ref[D_FF + c0:D_FF + c1, :], h, preferred_element_type=f32)
        act = (jax.nn.silu(gate) * up).astype(bf16)
        acc = acc + jnp.dot(wd_ref[:, c0:c1], act, preferred_element_type=f32)
    if len(rest) == 1:
        o_ref[...] = acc
    else:
        ms = jnp.mean(acc * acc, axis=0, keepdims=True)
        o_ref[...] = (acc * lax.rsqrt(ms + EPS) * rest[0][...]).T


def mix_ffn(xt, ya, yb, sga, sgb, wa_t, wb_t, wo_t, g2, wgu_t, wd_t, final_g=None):
    t = xt.shape[1]
    last = final_g is not None

    def fm(rows):
        return pl.BlockSpec((rows, TM), lambda i: (0, i))

    return pl.pallas_call(
        _mix_ffn_kernel,
        grid=(t // TM,),
        in_specs=[fm(D_MODEL), fm(A_WIDTH), fm(B_WIDTH), fm(D_MODEL), fm(D_MODEL),
                  _resident((D_MODEL, A_WIDTH)), _resident((D_MODEL, B_WIDTH)), _resident((D_MODEL, D_MODEL)),
                  _resident((D_MODEL, 1)), _resident((2 * D_FF, D_MODEL)), _resident((D_MODEL, D_FF))]
        + ([_resident((D_MODEL, 1))] if last else []),
        out_specs=pl.BlockSpec((TM, D_MODEL), lambda i: (i, 0)) if last else fm(D_MODEL),
        out_shape=jax.ShapeDtypeStruct((t, D_MODEL) if last else (D_MODEL, t), f32),
        compiler_params=_params("parallel"),
        name="mix_ffn",
    )(xt, ya, yb, sga, sgb, wa_t, wb_t, wo_t, g2.reshape(D_MODEL, 1), wgu_t, wd_t,
      *([final_g.reshape(D_MODEL, 1)] if last else []))


def _rope_tables(seq):
    t = jnp.arange(seq, dtype=jnp.int32)
    row = (t // GRID_W).astype(f32)
    col = (t % GRID_W).astype(f32)
    quarter = HEAD_DIM // 4
    inv = ROPE_THETA ** (-jnp.arange(quarter, dtype=f32) / quarter)
    ang_r = inv[:, None] * row[None, :]
    ang_c = inv[:, None] * col[None, :]
    cos = jnp.concatenate([jnp.cos(ang_r)] * 2 + [jnp.cos(ang_c)] * 2, axis=0)
    sin = jnp.concatenate([-jnp.sin(ang_r), jnp.sin(ang_r), -jnp.sin(ang_c), jnp.sin(ang_c)], axis=0)
    return cos, sin


def _trunk(x, weights, final_g):
    batch, seq, _ = x.shape
    assert seq % max(TM, TQ, TK) == 0 and seq % (NA_STEP_PAIRS * NA_PAIR) == 0 and seq // NA_PAIR >= NA_WIN_BLOCKS
    cos, sin = _rope_tables(seq)
    xt = to_feature_major(x.reshape(batch * seq, D_MODEL))
    for layer, w in enumerate(weights):
        qa, ka, va, qb, kb, vb, sga, sgb = in_proj(
            xt, w["norm1"], w["w_in_t"], w["q_norm"], w["k_norm"], cos, sin, seq)
        ya = global_attn(qa, ka, va, batch, seq)
        yb = nbr_attn(qb, kb, vb, w["na_bias"], batch, seq)
        xt = mix_ffn(xt, ya, yb, sga, sgb, w["w_up_a_t"], w["w_up_b_t"], w["w_out_t"],
                     w["norm2"], w["w_gate_up_t"], w["w_down_t"],
                     final_g if layer == len(weights) - 1 else None)
    return xt.reshape(batch, seq, D_MODEL)


def kernel(x_prompt, x_sample, norm1, w_in, q_norm, k_norm, rpb, w_up_a, w_up_b, w_out, norm2, w_gate_up,
           w_down, final_norm):
    depth = norm1.shape[0]

    def wt(w):
        return w.T.astype(bf16)

    na_bias = na_bias_tiles(rpb)
    weights = [dict(norm1=norm1[l], w_in_t=wt(w_in[l]), q_norm=q_norm[l], k_norm=k_norm[l], na_bias=na_bias[l],
                    w_up_a_t=wt(w_up_a[l]), w_up_b_t=wt(w_up_b[l]), w_out_t=wt(w_out[l]), norm2=norm2[l],
                    w_gate_up_t=wt(w_gate_up[l]), w_down_t=wt(w_down[l])) for l in range(depth)]
    return (_trunk(x_prompt, weights, final_norm), _trunk(x_sample, weights, final_norm))
```

```python
import numpy as np
import jax
import jax.numpy as jnp
from jax import lax
from jax.experimental import pallas as pl
from jax.experimental.pallas import tpu as pltpu

D_MODEL = 1024
GRID_W = 64
HEAD_DIM = 64
A_Q_HEADS = 8
A_KV_HEADS = 2
A_GROUPS = A_Q_HEADS // A_KV_HEADS
B_HEADS = 8
NA_ROWS = 8
NA_COLS = 16
ROPE_THETA = 10000.0
EPS = 1e-6
D_FF = -(-8 * D_MODEL // (3 * 256)) * 256
A_WIDTH = A_Q_HEADS * HEAD_DIM
A_KV_WIDTH = A_KV_HEADS * HEAD_DIM
B_WIDTH = B_HEADS * HEAD_DIM
QK_SCALE = HEAD_DIM ** -0.5
LOG2E = 1.4426950408889634
V_ROWS = HEAD_DIM + 16
NEG = -1e30

R_QA = 0
R_KA = R_QA + A_WIDTH
R_VA = R_KA + A_KV_WIDTH
R_QB = R_VA + A_KV_WIDTH
R_KB = R_QB + B_WIDTH
R_VB = R_KB + B_WIDTH
R_GA = R_VB + B_WIDTH
R_GB = R_GA + D_MODEL
IN_COLS = R_GB + D_MODEL

NA_PAIR = 2 * GRID_W
NA_WIN_BLOCKS = 5
NA_WIN = NA_WIN_BLOCKS * NA_PAIR
NA_CLASSES = 5

VMEM_LIMIT = 56 * 1024 * 1024

TM = 512
SUB = 256
TQ = 512
TK = 1024
CB = 256

f32 = jnp.float32
bf16 = jnp.bfloat16


def _params(*sem):
    return pltpu.CompilerParams(dimension_semantics=sem, vmem_limit_bytes=VMEM_LIMIT)


def _full(shape):
    return pl.BlockSpec(shape, lambda *_: (0,) * len(shape))


def _head_norm_rope(y, gain, cos, sin):
    nh = y.shape[0] // HEAD_DIM
    y = y.reshape(nh, HEAD_DIM, y.shape[1])
    ms = jnp.mean(y * y, axis=1, keepdims=True)
    y = y * lax.rsqrt(ms + EPS) * gain
    q = HEAD_DIM // 4
    rot = jnp.concatenate([y[:, q:2 * q], y[:, 0:q], y[:, 3 * q:4 * q], y[:, 2 * q:3 * q]], axis=1)
    y = y * cos + rot * sin
    return y.reshape(nh * HEAD_DIM, y.shape[2])


def _in_proj_kernel(x_ref, g1_ref, w_ref, qg_ref, kg_ref, cos_ref, sin_ref, *outs):
    if len(outs) == 9:
        xt_ref, *outs = outs
        x_all = x_ref[...].T
        xt_ref[...] = x_all
    else:
        x_all = x_ref[...]
    qa_ref, ka_ref, va_ref, qb_ref, kb_ref, vb_ref, sga_ref, sgb_ref = outs
    nsub = x_all.shape[1] // SUB
    hs = []
    for t in range(nsub):
        x = x_all[:, t * SUB:(t + 1) * SUB]
        ms = jnp.mean(x * x, axis=0, keepdims=True)
        hs.append((x * lax.rsqrt(ms + EPS) * g1_ref[...]).astype(bf16))
    for t in range(nsub):
        tc = slice(t * SUB, (t + 1) * SUB)
        h = hs[t]
        cos = cos_ref[:, tc]
        sin = sin_ref[:, tc]

        def proj(r0, r1, h=h):
            return jnp.dot(w_ref[r0:r1, :], h, preferred_element_type=f32)

        qa = _head_norm_rope(proj(R_QA, R_KA), qg_ref[...], cos, sin)
        qa_ref[:, tc] = (qa * (QK_SCALE * LOG2E)).astype(bf16)
        ka = _head_norm_rope(proj(R_KA, R_VA), kg_ref[...], cos, sin)
        ka_ref[tc, :] = ka.T.astype(bf16)
        va = proj(R_VA, R_QB).astype(bf16)
        ones = jnp.ones((V_ROWS - HEAD_DIM, SUB), bf16)
        va_ref[0, :, tc] = jnp.concatenate([va[0:HEAD_DIM], ones, va[HEAD_DIM:2 * HEAD_DIM], ones], axis=0)
        qb_ref[:, tc] = (proj(R_QB, R_KB) * (QK_SCALE * LOG2E)).astype(bf16)
        kb_ref[tc, :] = proj(R_KB, R_VB).T.astype(bf16)
        vb_ref[:, tc] = proj(R_VB, R_GA).astype(bf16)
        sga_ref[:, tc] = jax.nn.sigmoid(proj(R_GA, R_GB)).astype(bf16)
        sgb_ref[:, tc] = jax.nn.sigmoid(proj(R_GB, IN_COLS)).astype(bf16)


def in_proj(x, g1, w_in_t, qg, kg, cos, sin, seq, token_major_in=False):
    t = x.shape[0] if token_major_in else x.shape[1]
    n_seq_tiles = seq // TM
    tab = pl.BlockSpec((HEAD_DIM, TM), lambda i: (0, i % n_seq_tiles))

    def fm(rows):
        return pl.BlockSpec((rows, TM), lambda i: (0, i))

    def tok(cols):
        return pl.BlockSpec((TM, cols), lambda i: (i, 0))

    out_shapes = (
        jax.ShapeDtypeStruct((A_WIDTH, t), bf16),
        jax.ShapeDtypeStruct((t, A_KV_WIDTH), bf16),
        jax.ShapeDtypeStruct((t // TK, A_KV_HEADS * V_ROWS, TK), bf16),
        jax.ShapeDtypeStruct((B_WIDTH, t), bf16),
        jax.ShapeDtypeStruct((t, B_WIDTH), bf16),
        jax.ShapeDtypeStruct((B_WIDTH, t), bf16),
        jax.ShapeDtypeStruct((D_MODEL, t), bf16),
        jax.ShapeDtypeStruct((D_MODEL, t), bf16),
    )
    out_specs = (
        fm(A_WIDTH), tok(A_KV_WIDTH),
        pl.BlockSpec((1, A_KV_HEADS * V_ROWS, TM), lambda i: (i // (TK // TM), 0, i % (TK // TM))),
        fm(B_WIDTH), tok(B_WIDTH), fm(B_WIDTH), fm(D_MODEL), fm(D_MODEL),
    )
    if token_major_in:
        out_shapes = (jax.ShapeDtypeStruct((D_MODEL, t), f32),) + out_shapes
        out_specs = (fm(D_MODEL),) + out_specs
    return pl.pallas_call(
        _in_proj_kernel,
        grid=(t // TM,),
        in_specs=[tok(D_MODEL) if token_major_in else fm(D_MODEL), _full((D_MODEL, 1)), _full((IN_COLS, D_MODEL)),
                  _full((HEAD_DIM, 1)), _full((HEAD_DIM, 1)), tab, tab],
        out_specs=out_specs,
        out_shape=out_shapes,
        compiler_params=_params("parallel"),
        name="in_proj",
    )(x, g1.reshape(D_MODEL, 1), w_in_t, qg.reshape(HEAD_DIM, 1), kg.reshape(HEAD_DIM, 1), cos, sin)


def _global_attn_kernel(q_ref, k_ref, v_ref, o_ref, s_scr):
    tq = q_ref.shape[1]
    n_kv = v_ref.shape[0]
    tk = v_ref.shape[2]
    n = A_GROUPS * tq
    j = pl.program_id(1)
    q = jnp.concatenate([q_ref[g * HEAD_DIM:(g + 1) * HEAD_DIM, :] for g in range(A_GROUPS)], axis=1)
    zeros = jnp.zeros_like(q)
    qpad = jnp.concatenate([jnp.where(j == 0, q, zeros), jnp.where(j == 1, q, zeros)], axis=0)

    ncb = n // CB
    cols = [slice(c * CB, (c + 1) * CB) for c in range(ncb)]
    qblk = [qpad[:, c] for c in cols]

    def produce(i, slot, c):
        k = k_ref[pl.ds(pl.multiple_of(i * tk, tk), tk), :]
        s = jnp.dot(k, qblk[c], preferred_element_type=f32)
        s_scr[slot, :, cols[c]] = s
        return jnp.max(s, axis=0, keepdims=True)

    def consume(i, slot, c, mt, m, l, acc):
        m_new = jnp.maximum(m, mt)
        alpha = jnp.exp2(m - m_new)
        p = jnp.exp2(s_scr[slot, :, cols[c]] - m_new).astype(bf16)
        pv = jnp.dot(v_ref[i], p, preferred_element_type=f32)
        l = alpha * l + pv[HEAD_DIM:HEAD_DIM + 1, :]
        acc = alpha * acc + pv[0:HEAD_DIM, :]
        return m_new, l, acc

    def half(i_cur, slot_cur, state, i_next=None):
        out = []
        for c in range(ncb):
            mt, m, l, acc = state[c]
            mt_next = produce(i_next, 1 - slot_cur, c) if i_next is not None else mt
            out.append((mt_next,) + consume(i_cur, slot_cur, c, mt, m, l, acc))
        return out

    def body(it, state):
        i = 2 * it
        return half(i + 1, 1, half(i, 0, state, i + 1), i + 2)

    init = [(produce(0, 0, c), jnp.full((1, CB), NEG, f32), jnp.zeros((1, CB), f32), jnp.zeros((HEAD_DIM, CB), f32))
            for c in range(ncb)]
    state = lax.fori_loop(0, n_kv // 2 - 1, body, init)
    state = half(n_kv - 1, 1, half(n_kv - 2, 0, state, n_kv - 1))
    out = jnp.concatenate([(acc / l).astype(bf16) for _, _, l, acc in state], axis=1)
    for g in range(A_GROUPS):
        o_ref[g * HEAD_DIM:(g + 1) * HEAD_DIM, :] = out[:, g * tq:(g + 1) * tq]


def global_attn(qa, ka, va, batch, seq):
    t = batch * seq
    n_q = seq // TQ
    n_kv = seq // TK
    qo_spec = pl.BlockSpec((A_GROUPS * HEAD_DIM, TQ), lambda b, j, i: (j, b * n_q + i))
    return pl.pallas_call(
        _global_attn_kernel,
        grid=(batch, A_KV_HEADS, n_q),
        in_specs=[qo_spec,
                  pl.BlockSpec((seq, A_KV_WIDTH), lambda b, j, i: (b, 0)),
                  pl.BlockSpec((n_kv, V_ROWS, TK), lambda b, j, i: (b, j, 0))],
        out_specs=qo_spec,
        out_shape=jax.ShapeDtypeStruct((A_WIDTH, t), bf16),
        scratch_shapes=[pltpu.VMEM((2, TK, A_GROUPS * TQ), f32)],
        compiler_params=_params("parallel", "parallel", "parallel"),
        name="global_attn",
    )(qa, ka, va)


def _na_bias_index():
    x = np.arange(GRID_W)[:, None]
    c = np.arange(GRID_W)[None, :]
    cs = np.clip(c - NA_COLS // 2, 0, GRID_W - NA_COLS)
    col_ok = (x >= cs) & (x < cs + NA_COLS)
    dc = np.clip(x - c, -(NA_COLS - 1), NA_COLS - 1) + (NA_COLS - 1)
    onehot = (dc[None] == np.arange(2 * NA_COLS - 1)[:, None, None]).astype(np.float32)
    cls = [(0, (0, 0)), (2, (0, 0)), (4, (0, 1)), (6, (2, 2)), (8, (2, 2))]
    i = np.arange(NA_WIN_BLOCKS * 2)[:, None]
    u = np.arange(2)[None, :]
    rel, row_ok = [], []
    for delta, rs_off in cls:
        off = np.asarray(rs_off)[None, :]
        rel.append(np.clip(i - delta - u + (NA_ROWS - 1), 0, 2 * NA_ROWS - 2))
        row_ok.append((i - off >= 0) & (i - off < NA_ROWS))
    return onehot, col_ok, np.stack(rel).astype(np.int32), np.stack(row_ok)


def na_bias_tiles(rpb):
    onehot, col_ok, rel, row_ok = _na_bias_index()
    t1 = jnp.einsum("lhad,dxc->lhaxc", rpb.astype(f32), jnp.asarray(onehot), precision=lax.Precision.HIGHEST)
    g = jnp.take(t1, jnp.asarray(rel.reshape(-1)), axis=2)
    g = g.reshape(rpb.shape[0], B_HEADS, NA_CLASSES, NA_WIN_BLOCKS * 2, 2, GRID_W, GRID_W)
    ok = row_ok[:, :, :, None, None] & col_ok[None, None, None]
    g = jnp.where(jnp.asarray(ok)[None, None], g, NEG)
    g = g.reshape(rpb.shape[0], B_HEADS // 2, 2, NA_CLASSES, NA_WIN_BLOCKS * 2, 2, GRID_W, GRID_W)
    g = g.transpose(0, 3, 1, 4, 6, 2, 5, 7)
    return g.reshape(rpb.shape[0], NA_CLASSES, B_HEADS // 2, NA_WIN, 2 * NA_PAIR) * LOG2E


NA_STEP_PAIRS = 2


def _nbr_attn_kernel(q_ref, *refs):
    k_refs = refs[0:NA_STEP_PAIRS]
    v_refs = refs[NA_STEP_PAIRS:2 * NA_STEP_PAIRS]
    b_refs = refs[2 * NA_STEP_PAIRS:3 * NA_STEP_PAIRS]
    o_ref, s_scr, p_scr = refs[3 * NA_STEP_PAIRS:]
    zeros = jnp.zeros((HEAD_DIM, NA_PAIR), bf16)
    ones = jnp.ones((16, NA_WIN), bf16)
    n_hp = B_HEADS // 2
    units = [(e, hp) for e in range(NA_STEP_PAIRS) for hp in range(n_hp)]

    def produce(u):
        e, hp = units[u]
        rows = slice(hp * 2 * HEAD_DIM, (hp + 1) * 2 * HEAD_DIM)
        kwin = k_refs[e][:, rows]
        q2 = q_ref[rows, e * NA_PAIR:(e + 1) * NA_PAIR]
        qbd = jnp.concatenate([jnp.concatenate([q2[0:HEAD_DIM], zeros], axis=1),
                               jnp.concatenate([zeros, q2[HEAD_DIM:]], axis=1)], axis=0)
        mx = None
        for r in range(NA_WIN_BLOCKS):
            rr = slice(r * NA_PAIR, (r + 1) * NA_PAIR)
            x = jnp.dot(kwin[rr], qbd, preferred_element_type=f32) + b_refs[e][0, hp, rr, :]
            s_scr[u % 2, rr, :] = x
            xm = jnp.max(x, axis=0, keepdims=True)
            mx = xm if mx is None else jnp.maximum(mx, xm)
        return mx

    def softmax(u, m):
        p_scr[u % 2] = jnp.exp2(s_scr[u % 2] - m).astype(bf16)

    def weighted(u):
        e, hp = units[u]
        rows = slice(hp * 2 * HEAD_DIM, (hp + 1) * 2 * HEAD_DIM)
        vwin = jnp.concatenate([v_refs[e][rows, :], ones], axis=0)
        pv = jnp.dot(vwin, p_scr[u % 2], preferred_element_type=f32)
        o = pv[0:2 * HEAD_DIM] / pv[2 * HEAD_DIM:2 * HEAD_DIM + 1]
        cols = slice(e * NA_PAIR, (e + 1) * NA_PAIR)
        o_ref[hp * 2 * HEAD_DIM:hp * 2 * HEAD_DIM + HEAD_DIM, cols] = o[0:HEAD_DIM, 0:NA_PAIR].astype(bf16)
        o_ref[hp * 2 * HEAD_DIM + HEAD_DIM:(hp + 1) * 2 * HEAD_DIM, cols] = o[HEAD_DIM:, NA_PAIR:].astype(bf16)

    n_u = len(units)
    m = {0: produce(0)}
    m[1] = produce(1)
    softmax(0, m[0])
    for u in range(n_u):
        if u + 2 < n_u:
            m[u + 2] = produce(u + 2)
        if u + 1 < n_u:
            softmax(u + 1, m[u + 1])
        weighted(u)


def nbr_attn(qb, kb, vb, bias, batch, seq):
    t = batch * seq
    n_pairs = seq // NA_PAIR
    n_steps = n_pairs // NA_STEP_PAIRS
    last = n_pairs - NA_WIN_BLOCKS

    def win(p):
        return jnp.clip(p - 2, 0, last)

    def cls(p):
        return jnp.where(p < 2, p, jnp.where(p >= n_pairs - 2, p - (n_pairs - 2) + 3, 2))

    def k_spec(e):
        return pl.BlockSpec((pl.Element(NA_WIN), pl.Element(B_WIDTH)),
                            lambda b, g: ((b * n_pairs + win(NA_STEP_PAIRS * g + e)) * NA_PAIR, 0))

    def v_spec(e):
        return pl.BlockSpec((pl.Element(B_WIDTH), pl.Element(NA_WIN)),
                            lambda b, g: (0, (b * n_pairs + win(NA_STEP_PAIRS * g + e)) * NA_PAIR))

    def b_spec(e):
        return pl.BlockSpec((1, B_HEADS // 2, NA_WIN, 2 * NA_PAIR), lambda b, g: (cls(NA_STEP_PAIRS * g + e), 0, 0, 0))

    qo_spec = pl.BlockSpec((B_WIDTH, NA_STEP_PAIRS * NA_PAIR), lambda b, g: (0, b * n_steps + g))
    es = range(NA_STEP_PAIRS)
    return pl.pallas_call(
        _nbr_attn_kernel,
        grid=(batch, n_steps),
        in_specs=[qo_spec] + [k_spec(e) for e in es] + [v_spec(e) for e in es] + [b_spec(e) for e in es],
        out_specs=qo_spec,
        out_shape=jax.ShapeDtypeStruct((B_WIDTH, t), bf16),
        scratch_shapes=[pltpu.VMEM((2, NA_WIN, 2 * NA_PAIR), f32), pltpu.VMEM((2, NA_WIN, 2 * NA_PAIR), bf16)],
        compiler_params=_params("parallel", "arbitrary"),
        name="nbr_attn",
    )(qb, *([kb] * NA_STEP_PAIRS), *([vb] * NA_STEP_PAIRS), *([bias] * NA_STEP_PAIRS))


def _resident(shape):
    return pl.BlockSpec(shape, lambda *_: (0,) * len(shape), pipeline_mode=pl.Buffered(1))


FF_CHUNKS = ((0, 1024), (1024, 2048), (2048, D_FF))


def _mix_ffn_kernel(x_ref, ya_ref, yb_ref, sga_ref, sgb_ref, wa_ref, wb_ref, wo_ref, g2_ref, wgu_ref, wd_ref, *rest):
    o_ref = rest[-1]
    ua = jnp.dot(wa_ref[...], ya_ref[...], preferred_element_type=f32)
    ub = jnp.dot(wb_ref[...], yb_ref[...], preferred_element_type=f32)
    mix = sga_ref[...].astype(f32) * ua + sgb_ref[...].astype(f32) * ub
    x = x_ref[...] + jnp.dot(wo_ref[...], mix.astype(bf16), preferred_element_type=f32)
    ms = jnp.mean(x * x, axis=0, keepdims=True)
    h = (x * lax.rsqrt(ms + EPS) * g2_ref[...]).astype(bf16)
    acc = x
    for c0, c1 in FF_CHUNKS:
        gate = jnp.dot(wgu_ref[c0:c1, :], h, preferred_element_type=f32)
        up = jnp.dot(wgu_ref[D_FF + c0:D_FF + c1, :], h, preferred_element_type=f32)
        act = (jax.nn.silu(gate) * up).astype(bf16)
        acc = acc + jnp.dot(wd_ref[:, c0:c1], act, preferred_element_type=f32)
    if len(rest) == 1:
        o_ref[...] = acc
    else:
        ms = jnp.mean(acc * acc, axis=0, keepdims=True)
        o_ref[...] = (acc * lax.rsqrt(ms + EPS) * rest[0][...]).T


def mix_ffn(xt, ya, yb, sga, sgb, wa_t, wb_t, wo_t, g2, wgu_t, wd_t, final_g=None):
    t = xt.shape[1]
    last = final_g is not None

    def fm(rows):
        return pl.BlockSpec((rows, TM), lambda i: (0, i))

    return pl.pallas_call(
        _mix_ffn_kernel,
        grid=(t // TM,),
        in_specs=[fm(D_MODEL), fm(A_WIDTH), fm(B_WIDTH), fm(D_MODEL), fm(D_MODEL),
                  _resident((D_MODEL, A_WIDTH)), _resident((D_MODEL, B_WIDTH)), _resident((D_MODEL, D_MODEL)),
                  _resident((D_MODEL, 1)), _resident((2 * D_FF, D_MODEL)), _resident((D_MODEL, D_FF))]
        + ([_resident((D_MODEL, 1))] if last else []),
        out_specs=pl.BlockSpec((TM, D_MODEL), lambda i: (i, 0)) if last else fm(D_MODEL),
        out_shape=jax.ShapeDtypeStruct((t, D_MODEL) if last else (D_MODEL, t), f32),
        compiler_params=_params("parallel"),
        name="mix_ffn",
    )(xt, ya, yb, sga, sgb, wa_t, wb_t, wo_t, g2.reshape(D_MODEL, 1), wgu_t, wd_t,
      *([final_g.reshape(D_MODEL, 1)] if last else []))


W_COLS = 256


def _weight_t_kernel(w_ref, o_ref):
    o_ref[0] = w_ref[0].T.astype(bf16)


def weight_t(w):
    depth, n_in, n_out = w.shape
    return pl.pallas_call(
        _weight_t_kernel,
        grid=(depth, n_out // W_COLS),
        in_specs=[pl.BlockSpec((1, n_in, W_COLS), lambda l, j: (l, 0, j))],
        out_specs=pl.BlockSpec((1, W_COLS, n_in), lambda l, j: (l, j, 0)),
        out_shape=jax.ShapeDtypeStruct((depth, n_out, n_in), bf16),
        compiler_params=_params("parallel", "parallel"),
        name="weight_t",
    )(w)


def _rope_tables(seq):
    t = jnp.arange(seq, dtype=jnp.int32)
    row = (t // GRID_W).astype(f32)
    col = (t % GRID_W).astype(f32)
    quarter = HEAD_DIM // 4
    inv = ROPE_THETA ** (-jnp.arange(quarter, dtype=f32) / quarter)
    ang_r = inv[:, None] * row[None, :]
    ang_c = inv[:, None] * col[None, :]
    cos = jnp.concatenate([jnp.cos(ang_r)] * 2 + [jnp.cos(ang_c)] * 2, axis=0)
    sin = jnp.concatenate([-jnp.sin(ang_r), jnp.sin(ang_r), -jnp.sin(ang_c), jnp.sin(ang_c)], axis=0)
    return cos, sin


def _trunk(x, weights, final_g):
    batch, seq, _ = x.shape
    assert seq % max(TM, TQ, TK) == 0 and seq % (NA_STEP_PAIRS * NA_PAIR) == 0 and seq // NA_PAIR >= NA_WIN_BLOCKS
    cos, sin = _rope_tables(seq)
    xt = x.reshape(batch * seq, D_MODEL)
    for layer, w in enumerate(weights):
        outs = in_proj(xt, w["norm1"], w["w_in_t"], w["q_norm"], w["k_norm"], cos, sin, seq,
                       token_major_in=layer == 0)
        if layer == 0:
            xt, *outs = outs
        qa, ka, va, qb, kb, vb, sga, sgb = outs
        ya = global_attn(qa, ka, va, batch, seq)
        yb = nbr_attn(qb, kb, vb, w["na_bias"], batch, seq)
        xt = mix_ffn(xt, ya, yb, sga, sgb, w["w_up_a_t"], w["w_up_b_t"], w["w_out_t"],
                     w["norm2"], w["w_gate_up_t"], w["w_down_t"],
                     final_g if layer == len(weights) - 1 else None)
    return xt.reshape(batch, seq, D_MODEL)


def kernel(x_prompt, x_sample, norm1, w_in, q_norm, k_norm, rpb, w_up_a, w_up_b, w_out, norm2, w_gate_up,
           w_down, final_norm):
    depth = norm1.shape[0]

    w_in_t, w_up_a_t, w_up_b_t, w_out_t, w_gate_up_t, w_down_t = (
        weight_t(w) for w in (w_in, w_up_a, w_up_b, w_out, w_gate_up, w_down))
    na_bias = na_bias_tiles(rpb)
    weights = [dict(norm1=norm1[l], w_in_t=w_in_t[l], q_norm=q_norm[l], k_norm=k_norm[l], na_bias=na_bias[l],
                    w_up_a_t=w_up_a_t[l], w_up_b_t=w_up_b_t[l], w_out_t=w_out_t[l], norm2=norm2[l],
                    w_gate_up_t=w_gate_up_t[l], w_down_t=w_down_t[l]) for l in range(depth)]
    return (_trunk(x_prompt, weights, final_norm), _trunk(x_sample, weights, final_norm))
```

```python
import numpy as np
import jax
import jax.numpy as jnp
from jax import lax
from jax.experimental import pallas as pl
from jax.experimental.pallas import tpu as pltpu

D_MODEL = 1024
GRID_W = 64
HEAD_DIM = 64
A_Q_HEADS = 8
A_KV_HEADS = 2
A_GROUPS = A_Q_HEADS // A_KV_HEADS
B_HEADS = 8
NA_ROWS = 8
NA_COLS = 16
ROPE_THETA = 10000.0
EPS = 1e-6
D_FF = -(-8 * D_MODEL // (3 * 256)) * 256
A_WIDTH = A_Q_HEADS * HEAD_DIM
A_KV_WIDTH = A_KV_HEADS * HEAD_DIM
B_WIDTH = B_HEADS * HEAD_DIM
QK_SCALE = HEAD_DIM ** -0.5
LOG2E = 1.4426950408889634
V_ROWS = HEAD_DIM + 16
NEG = -1e30

R_QA = 0
R_KA = R_QA + A_WIDTH
R_VA = R_KA + A_KV_WIDTH
R_QB = R_VA + A_KV_WIDTH
R_KB = R_QB + B_WIDTH
R_VB = R_KB + B_WIDTH
R_GA = R_VB + B_WIDTH
R_GB = R_GA + D_MODEL
IN_COLS = R_GB + D_MODEL

NA_PAIR = 2 * GRID_W
NA_WIN_BLOCKS = 5
NA_WIN = NA_WIN_BLOCKS * NA_PAIR
NA_CLASSES = 5

VMEM_LIMIT = 56 * 1024 * 1024

TM = 512
SUB = 256
TQ = 512
TK = 1024
CB = 256
QG = 8

f32 = jnp.float32
bf16 = jnp.bfloat16


def _params(*sem):
    return pltpu.CompilerParams(dimension_semantics=sem, vmem_limit_bytes=VMEM_LIMIT)


def _full(shape):
    return pl.BlockSpec(shape, lambda *_: (0,) * len(shape))


def _head_norm_rope(y, gain, cos, sin):
    nh = y.shape[0] // HEAD_DIM
    y = y.reshape(nh, HEAD_DIM, y.shape[1])
    ms = jnp.mean(y * y, axis=1, keepdims=True)
    y = y * lax.rsqrt(ms + EPS) * gain
    q = HEAD_DIM // 4
    rot = jnp.concatenate([y[:, q:2 * q], y[:, 0:q], y[:, 3 * q:4 * q], y[:, 2 * q:3 * q]], axis=1)
    y = y * cos + rot * sin
    return y.reshape(nh * HEAD_DIM, y.shape[2])


def _in_proj_kernel(x_ref, g1_ref, w_ref, qg_ref, kg_ref, cos_ref, sin_ref, *outs):
    if len(outs) == 9:
        xt_ref, *outs = outs
        x_all = x_ref[...].T
        xt_ref[...] = x_all
    else:
        x_all = x_ref[...]
    qa_ref, ka_ref, va_ref, qb_ref, kb_ref, vb_ref, sga_ref, sgb_ref = outs
    nsub = x_all.shape[1] // SUB
    hs = []
    for t in range(nsub):
        x = x_all[:, t * SUB:(t + 1) * SUB]
        ms = jnp.mean(x * x, axis=0, keepdims=True)
        hs.append((x * lax.rsqrt(ms + EPS) * g1_ref[...]).astype(bf16))
    for t in range(nsub):
        tc = slice(t * SUB, (t + 1) * SUB)
        h = hs[t]
        cos = cos_ref[:, tc]
        sin = sin_ref[:, tc]

        def proj(r0, r1, h=h):
            return jnp.dot(w_ref[r0:r1, :], h, preferred_element_type=f32)

        qa = _head_norm_rope(proj(R_QA, R_KA), qg_ref[...], cos, sin)
        qa_ref[0, :, tc] = (qa * (QK_SCALE * LOG2E)).astype(bf16)
        ka = _head_norm_rope(proj(R_KA, R_VA), kg_ref[...], cos, sin)
        ka_ref[tc, :] = ka.T.astype(bf16)
        va = proj(R_VA, R_QB).astype(bf16)
        ones = jnp.ones((V_ROWS - HEAD_DIM, SUB), bf16)
        va_ref[0, :, tc] = jnp.concatenate([va[0:HEAD_DIM], ones, va[HEAD_DIM:2 * HEAD_DIM], ones], axis=0)
        qb_ref[:, tc] = (proj(R_QB, R_KB) * (QK_SCALE * LOG2E)).astype(bf16)
        kb_ref[tc, :] = proj(R_KB, R_VB).T.astype(bf16)
        vb_ref[:, tc] = proj(R_VB, R_GA).astype(bf16)
        sga_ref[:, tc] = jax.nn.sigmoid(proj(R_GA, R_GB)).astype(bf16)
        sgb_ref[:, tc] = jax.nn.sigmoid(proj(R_GB, IN_COLS)).astype(bf16)


def in_proj(x, g1, w_in_t, qg, kg, cos, sin, seq, token_major_in=False):
    t = x.shape[0] if token_major_in else x.shape[1]
    n_seq_tiles = seq // TM
    tab = pl.BlockSpec((HEAD_DIM, TM), lambda i: (0, i % n_seq_tiles))

    def fm(rows):
        return pl.BlockSpec((rows, TM), lambda i: (0, i))

    def tok(cols):
        return pl.BlockSpec((TM, cols), lambda i: (i, 0))

    out_shapes = (
        jax.ShapeDtypeStruct((t // TM, A_WIDTH, TM), bf16),
        jax.ShapeDtypeStruct((t, A_KV_WIDTH), bf16),
        jax.ShapeDtypeStruct((t // TK, A_KV_HEADS * V_ROWS, TK), bf16),
        jax.ShapeDtypeStruct((B_WIDTH, t), bf16),
        jax.ShapeDtypeStruct((t, B_WIDTH), bf16),
        jax.ShapeDtypeStruct((B_WIDTH, t), bf16),
        jax.ShapeDtypeStruct((D_MODEL, t), bf16),
        jax.ShapeDtypeStruct((D_MODEL, t), bf16),
    )
    out_specs = (
        pl.BlockSpec((1, A_WIDTH, TM), lambda i: (i, 0, 0)), tok(A_KV_WIDTH),
        pl.BlockSpec((1, A_KV_HEADS * V_ROWS, TM), lambda i: (i // (TK // TM), 0, i % (TK // TM))),
        fm(B_WIDTH), tok(B_WIDTH), fm(B_WIDTH), fm(D_MODEL), fm(D_MODEL),
    )
    if token_major_in:
        out_shapes = (jax.ShapeDtypeStruct((D_MODEL, t), f32),) + out_shapes
        out_specs = (fm(D_MODEL),) + out_specs
    return pl.pallas_call(
        _in_proj_kernel,
        grid=(t // TM,),
        in_specs=[tok(D_MODEL) if token_major_in else fm(D_MODEL), _full((D_MODEL, 1)), _full((IN_COLS, D_MODEL)),
                  _full((HEAD_DIM, 1)), _full((HEAD_DIM, 1)), tab, tab],
        out_specs=out_specs,
        out_shape=out_shapes,
        compiler_params=_params("parallel"),
        name="in_proj",
    )(x, g1.reshape(D_MODEL, 1), w_in_t, qg.reshape(HEAD_DIM, 1), kg.reshape(HEAD_DIM, 1), cos, sin)


def _global_attn_kernel(q_ref, k_ref, v_ref, o_ref, s_scr):
    n_q, _, tq = q_ref.shape
    n_kv = v_ref.shape[0]
    tk = v_ref.shape[2]
    n = A_GROUPS * tq
    j = pl.program_id(1)
    ncb = n // CB
    cols = [slice(c * CB, (c + 1) * CB) for c in range(ncb)]

    def load_q(qi):
        qt = q_ref[qi]
        q = jnp.concatenate([qt[g * HEAD_DIM:(g + 1) * HEAD_DIM, :] for g in range(A_GROUPS)], axis=1)
        zeros = jnp.zeros_like(q)
        qpad = jnp.concatenate([jnp.where(j == 0, q, zeros), jnp.where(j == 1, q, zeros)], axis=0)
        return [qpad[:, c] for c in cols]

    def produce(i, slot, c, qblk):
        k = k_ref[pl.ds(pl.multiple_of(i * tk, tk), tk), :]
        s = jnp.dot(k, qblk[c], preferred_element_type=f32)
        s_scr[slot, :, cols[c]] = s
        return jnp.max(s, axis=0, keepdims=True)

    def consume(i, slot, c, mt, m, l, acc):
        m_new = jnp.maximum(m, mt)
        alpha = jnp.exp2(m - m_new)
        p = jnp.exp2(s_scr[slot, :, cols[c]] - m_new).astype(bf16)
        pv = jnp.dot(v_ref[i], p, preferred_element_type=f32)
        l = alpha * l + pv[HEAD_DIM:HEAD_DIM + 1, :]
        acc = alpha * acc + pv[0:HEAD_DIM, :]
        return m_new, l, acc

    def half(i_cur, slot_cur, state, qblk, i_next):
        out = []
        for c in range(ncb):
            mt, m, l, acc = state[c]
            mt_next = produce(i_next, 1 - slot_cur, c, qblk)
            out.append((mt_next,) + consume(i_cur, slot_cur, c, mt, m, l, acc))
        return out

    def q_tile(qi, mts):
        qblk = load_q(qi)
        state = [(mts[c], jnp.full((1, CB), NEG, f32), jnp.zeros((1, CB), f32), jnp.zeros((HEAD_DIM, CB), f32))
                 for c in range(ncb)]

        def body(it, state):
            i = 2 * it
            return half(i + 1, 1, half(i, 0, state, qblk, i + 1), qblk, i + 2)

        state = lax.fori_loop(0, n_kv // 2 - 1, body, state)
        state = half(n_kv - 2, 0, state, qblk, n_kv - 1)
        state = half(n_kv - 1, 1, state, load_q(jnp.minimum(qi + 1, n_q - 1)), 0)
        out = jnp.concatenate([(acc / l).astype(bf16) for _, _, l, acc in state], axis=1)
        o_ref[qi] = jnp.concatenate([out[:, g * tq:(g + 1) * tq] for g in range(A_GROUPS)], axis=0)
        return [st[0] for st in state]

    qblk0 = load_q(0)
    lax.fori_loop(0, n_q, q_tile, [produce(0, 0, c, qblk0) for c in range(ncb)])


def global_attn(qa, ka, va, batch, seq):
    n_g = seq // (QG * TQ)
    n_kv = seq // TK
    qo_spec = pl.BlockSpec((QG, A_GROUPS * HEAD_DIM, TQ), lambda b, j, g: (b * n_g + g, j, 0))
    return pl.pallas_call(
        _global_attn_kernel,
        grid=(batch, A_KV_HEADS, n_g),
        in_specs=[qo_spec,
                  pl.BlockSpec((seq, A_KV_WIDTH), lambda b, j, g: (b, 0)),
                  pl.BlockSpec((n_kv, V_ROWS, TK), lambda b, j, g: (b, j, 0))],
        out_specs=qo_spec,
        out_shape=jax.ShapeDtypeStruct(qa.shape, bf16),
        scratch_shapes=[pltpu.VMEM((2, TK, A_GROUPS * TQ), f32)],
        compiler_params=_params("parallel", "parallel", "parallel"),
        name="global_attn",
    )(qa, ka, va)


def _na_bias_index():
    x = np.arange(GRID_W)[:, None]
    c = np.arange(GRID_W)[None, :]
    cs = np.clip(c - NA_COLS // 2, 0, GRID_W - NA_COLS)
    col_ok = (x >= cs) & (x < cs + NA_COLS)
    dc = np.clip(x - c, -(NA_COLS - 1), NA_COLS - 1) + (NA_COLS - 1)
    onehot = (dc[None] == np.arange(2 * NA_COLS - 1)[:, None, None]).astype(np.float32)
    cls = [(0, (0, 0)), (2, (0, 0)), (4, (0, 1)), (6, (2, 2)), (8, (2, 2))]
    i = np.arange(NA_WIN_BLOCKS * 2)[:, None]
    u = np.arange(2)[None, :]
    rel, row_ok = [], []
    for delta, rs_off in cls:
        off = np.asarray(rs_off)[None, :]
        rel.append(np.clip(i - delta - u + (NA_ROWS - 1), 0, 2 * NA_ROWS - 2))
        row_ok.append((i - off >= 0) & (i - off < NA_ROWS))
    return onehot, col_ok, np.stack(rel).astype(np.int32), np.stack(row_ok)


def na_bias_tiles(rpb):
    onehot, col_ok, rel, row_ok = _na_bias_index()
    depth = rpb.shape[0]
    t1 = jnp.einsum("lhad,dxc->lhaxc", rpb.astype(f32), jnp.asarray(onehot), precision=lax.Precision.HIGHEST)
    parts = []
    for e in range(2):
        for u in range(2):
            g = jnp.take(t1[:, e::2], jnp.asarray(rel[:, :, u].reshape(-1)), axis=2)
            ok = (row_ok[:, :, u, None, None] & col_ok[None, None]).reshape(-1, GRID_W, GRID_W)
            parts.append(jnp.where(jnp.asarray(ok)[None, None], g, NEG))
    g = jnp.concatenate(parts, axis=-1)
    return g.reshape(depth, B_HEADS // 2, NA_CLASSES, NA_WIN, 2 * NA_PAIR) * LOG2E


NA_STEP_PAIRS = 2


def _nbr_attn_kernel(q_ref, *refs):
    k_refs = refs[0:NA_STEP_PAIRS]
    v_refs = refs[NA_STEP_PAIRS:2 * NA_STEP_PAIRS]
    b_refs = refs[2 * NA_STEP_PAIRS:3 * NA_STEP_PAIRS]
    o_ref, s_scr, p_scr = refs[3 * NA_STEP_PAIRS:]
    zeros = jnp.zeros((HEAD_DIM, NA_PAIR), bf16)
    ones = jnp.ones((16, NA_WIN), bf16)
    n_hp = B_HEADS // 2
    units = [(e, hp) for e in range(NA_STEP_PAIRS) for hp in range(n_hp)]

    def produce(u):
        e, hp = units[u]
        rows = slice(hp * 2 * HEAD_DIM, (hp + 1) * 2 * HEAD_DIM)
        kwin = k_refs[e][:, rows]
        q2 = q_ref[rows, e * NA_PAIR:(e + 1) * NA_PAIR]
        qbd = jnp.concatenate([jnp.concatenate([q2[0:HEAD_DIM], zeros], axis=1),
                               jnp.concatenate([zeros, q2[HEAD_DIM:]], axis=1)], axis=0)
        mx = None
        for r in range(NA_WIN_BLOCKS):
            rr = slice(r * NA_PAIR, (r + 1) * NA_PAIR)
            x = jnp.dot(kwin[rr], qbd, preferred_element_type=f32) + b_refs[e][hp, 0, rr, :]
            s_scr[u % 2, rr, :] = x
            xm = jnp.max(x, axis=0, keepdims=True)
            mx = xm if mx is None else jnp.maximum(mx, xm)
        return mx

    def softmax(u, m):
        p_scr[u % 2] = jnp.exp2(s_scr[u % 2] - m).astype(bf16)

    def weighted(u):
        e, hp = units[u]
        rows = slice(hp * 2 * HEAD_DIM, (hp + 1) * 2 * HEAD_DIM)
        vwin = jnp.concatenate([v_refs[e][rows, :], ones], axis=0)
        pv = jnp.dot(vwin, p_scr[u % 2], preferred_element_type=f32)
        o = pv[0:2 * HEAD_DIM] / pv[2 * HEAD_DIM:2 * HEAD_DIM + 1]
        cols = slice(e * NA_PAIR, (e + 1) * NA_PAIR)
        o_ref[hp * 2 * HEAD_DIM:hp * 2 * HEAD_DIM + HEAD_DIM, cols] = o[0:HEAD_DIM, 0:NA_PAIR].astype(bf16)
        o_ref[hp * 2 * HEAD_DIM + HEAD_DIM:(hp + 1) * 2 * HEAD_DIM, cols] = o[HEAD_DIM:, NA_PAIR:].astype(bf16)

    n_u = len(units)
    m = {0: produce(0)}
    m[1] = produce(1)
    softmax(0, m[0])
    for u in range(n_u):
        if u + 2 < n_u:
            m[u + 2] = produce(u + 2)
        if u + 1 < n_u:
            softmax(u + 1, m[u + 1])
        weighted(u)


def nbr_attn(qb, kb, vb, bias, batch, seq):
    t = batch * seq
    n_pairs = seq // NA_PAIR
    n_steps = n_pairs // NA_STEP_PAIRS
    last = n_pairs - NA_WIN_BLOCKS

    def win(p):
        return jnp.clip(p - 2, 0, last)

    def cls(p):
        return jnp.where(p < 2, p, jnp.where(p >= n_pairs - 2, p - (n_pairs - 2) + 3, 2))

    def k_spec(e):
        return pl.BlockSpec((pl.Element(NA_WIN), pl.Element(B_WIDTH)),
                            lambda b, g: ((b * n_pairs + win(NA_STEP_PAIRS * g + e)) * NA_PAIR, 0))

    def v_spec(e):
        return pl.BlockSpec((pl.Element(B_WIDTH), pl.Element(NA_WIN)),
                            lambda b, g: (0, (b * n_pairs + win(NA_STEP_PAIRS * g + e)) * NA_PAIR))

    def b_spec(e):
        return pl.BlockSpec((B_HEADS // 2, 1, NA_WIN, 2 * NA_PAIR), lambda b, g: (0, cls(NA_STEP_PAIRS * g + e), 0, 0))

    qo_spec = pl.BlockSpec((B_WIDTH, NA_STEP_PAIRS * NA_PAIR), lambda b, g: (0, b * n_steps + g))
    es = range(NA_STEP_PAIRS)
    return pl.pallas_call(
        _nbr_attn_kernel,
        grid=(batch, n_steps),
        in_specs=[qo_spec] + [k_spec(e) for e in es] + [v_spec(e) for e in es] + [b_spec(e) for e in es],
        out_specs=qo_spec,
        out_shape=jax.ShapeDtypeStruct((B_WIDTH, t), bf16),
        scratch_shapes=[pltpu.VMEM((2, NA_WIN, 2 * NA_PAIR), f32), pltpu.VMEM((2, NA_WIN, 2 * NA_PAIR), bf16)],
        compiler_params=_params("parallel", "arbitrary"),
        name="nbr_attn",
    )(qb, *([kb] * NA_STEP_PAIRS), *([vb] * NA_STEP_PAIRS), *([bias] * NA_STEP_PAIRS))


def _resident(shape):
    return pl.BlockSpec(shape, lambda *_: (0,) * len(shape), pipeline_mode=pl.Buffered(1))


FF_CHUNKS = ((0, 1024), (1024, 2048), (2048, D_FF))


def _mix_ffn_kernel(x_ref, ya_ref, yb_ref, sga_ref, sgb_ref, wa_ref, wb_ref, wo_ref, g2_ref, wgu_ref, wd_ref, *rest):
    o_ref = rest[-1]
    ua = jnp.dot(wa_ref[...], ya_ref[0], preferred_element_type=f32)
    ub = jnp.dot(wb_ref[...], yb_ref[...], preferred_element_type=f32)
    mix = sga_ref[...].astype(f32) * ua + sgb_ref[...].astype(f32) * ub
    x = x_ref[...] + jnp.dot(wo_ref[...], mix.astype(bf16), preferred_element_type=f32)
    ms = jnp.mean(x * x, axis=0, keepdims=True)
    h = (x * lax.rsqrt(ms + EPS) * g2_ref[...]).astype(bf16)
    acc = x
    for c0, c1 in FF_CHUNKS:
        gate = jnp.dot(wgu_ref[c0:c1, :], h, preferred_element_type=f32)
        up = jnp.dot(wgu_ref[D_FF + c0:D_FF + c1, :], h, preferred_element_type=f32)
        act = (jax.nn.silu(gate) * up).astype(bf16)
        acc = acc + jnp.dot(wd_ref[:, c0:c1], act, preferred_element_type=f32)
    if len(rest) == 1:
        o_ref[...] = acc
    else:
        ms = jnp.mean(acc * acc, axis=0, keepdims=True)
        o_ref[...] = (acc * lax.rsqrt(ms + EPS) * rest[0][...]).T


def mix_ffn(xt, ya, yb, sga, sgb, wa_t, wb_t, wo_t, g2, wgu_t, wd_t, final_g=None):
    t = xt.shape[1]
    last = final_g is not None

    def fm(rows):
        return pl.BlockSpec((rows, TM), lambda i: (0, i))

    return pl.pallas_call(
        _mix_ffn_kernel,
        grid=(t // TM,),
        in_specs=[fm(D_MODEL), pl.BlockSpec((1, A_WIDTH, TM), lambda i: (i, 0, 0)), fm(B_WIDTH), fm(D_MODEL), fm(D_MODEL),
                  _resident((D_MODEL, A_WIDTH)), _resident((D_MODEL, B_WIDTH)), _resident((D_MODEL, D_MODEL)),
                  _resident((D_MODEL, 1)), _resident((2 * D_FF, D_MODEL)), _resident((D_MODEL, D_FF))]
        + ([_resident((D_MODEL, 1))] if last else []),
        out_specs=pl.BlockSpec((TM, D_MODEL), lambda i: (i, 0)) if last else fm(D_MODEL),
        out_shape=jax.ShapeDtypeStruct((t, D_MODEL) if last else (D_MODEL, t), f32),
        compiler_params=_params("parallel"),
        name="mix_ffn",
    )(xt, ya, yb, sga, sgb, wa_t, wb_t, wo_t, g2.reshape(D_MODEL, 1), wgu_t, wd_t,
      *([final_g.reshape(D_MODEL, 1)] if last else []))


W_COLS = 256


def _weight_t_kernel(w_ref, o_ref):
    o_ref[0] = w_ref[0].T.astype(bf16)


def weight_t(w):
    depth, n_in, n_out = w.shape
    return pl.pallas_call(
        _weight_t_kernel,
        grid=(depth, n_out // W_COLS),
        in_specs=[pl.BlockSpec((1, n_in, W_COLS), lambda l, j: (l, 0, j))],
        out_specs=pl.BlockSpec((1, W_COLS, n_in), lambda l, j: (l, j, 0)),
        out_shape=jax.ShapeDtypeStruct((depth, n_out, n_in), bf16),
        compiler_params=_params("parallel", "parallel"),
        name="weight_t",
    )(w)


def _rope_tables(seq):
    t = jnp.arange(seq, dtype=jnp.int32)
    row = (t // GRID_W).astype(f32)
    col = (t % GRID_W).astype(f32)
    quarter = HEAD_DIM // 4
    inv = ROPE_THETA ** (-jnp.arange(quarter, dtype=f32) / quarter)
    ang_r = inv[:, None] * row[None, :]
    ang_c = inv[:, None] * col[None, :]
    cos = jnp.concatenate([jnp.cos(ang_r)] * 2 + [jnp.cos(ang_c)] * 2, axis=0)
    sin = jnp.concatenate([-jnp.sin(ang_r), jnp.sin(ang_r), -jnp.sin(ang_c), jnp.sin(ang_c)], axis=0)
    return cos, sin


def _trunk(x, weights, final_g):
    batch, seq, _ = x.shape
    assert TM == TQ and seq % max(QG * TQ, TK) == 0 and seq % (NA_STEP_PAIRS * NA_PAIR) == 0 and seq // NA_PAIR >= NA_WIN_BLOCKS
    cos, sin = _rope_tables(seq)
    xt = x.reshape(batch * seq, D_MODEL)
    for layer, w in enumerate(weights):
        outs = in_proj(xt, w["norm1"], w["w_in_t"], w["q_norm"], w["k_norm"], cos, sin, seq,
                       token_major_in=layer == 0)
        if layer == 0:
            xt, *outs = outs
        qa, ka, va, qb, kb, vb, sga, sgb = outs
        ya = global_attn(qa, ka, va, batch, seq)
        yb = nbr_attn(qb, kb, vb, w["na_bias"], batch, seq)
        xt = mix_ffn(xt, ya, yb, sga, sgb, w["w_up_a_t"], w["w_up_b_t"], w["w_out_t"],
                     w["norm2"], w["w_gate_up_t"], w["w_down_t"],
                     final_g if layer == len(weights) - 1 else None)
    return xt.reshape(batch, seq, D_MODEL)


def kernel(x_prompt, x_sample, norm1, w_in, q_norm, k_norm, rpb, w_up_a, w_up_b, w_out, norm2, w_gate_up,
           w_down, final_norm):
    depth = norm1.shape[0]

    w_in_t, w_up_a_t, w_up_b_t, w_out_t, w_gate_up_t, w_down_t = (
        weight_t(w) for w in (w_in, w_up_a, w_up_b, w_out, w_gate_up, w_down))
    na_bias = na_bias_tiles(rpb)
    weights = [dict(norm1=norm1[l], w_in_t=w_in_t[l], q_norm=q_norm[l], k_norm=k_norm[l], na_bias=na_bias[l],
                    w_up_a_t=w_up_a_t[l], w_up_b_t=w_up_b_t[l], w_out_t=w_out_t[l], norm2=norm2[l],
                    w_gate_up_t=w_gate_up_t[l], w_down_t=w_down_t[l]) for l in range(depth)]
    return (_trunk(x_prompt, weights, final_norm), _trunk(x_sample, weights, final_norm))
```

```python
import numpy as np
import jax
import jax.numpy as jnp
from jax import lax
from jax.experimental import pallas as pl
from jax.experimental.pallas import tpu as pltpu

D_MODEL = 1024
GRID_W = 64
HEAD_DIM = 64
A_Q_HEADS = 8
A_KV_HEADS = 2
A_GROUPS = A_Q_HEADS // A_KV_HEADS
B_HEADS = 8
NA_ROWS = 8
NA_COLS = 16
ROPE_THETA = 10000.0
EPS = 1e-6
D_FF = -(-8 * D_MODEL // (3 * 256)) * 256
A_WIDTH = A_Q_HEADS * HEAD_DIM
A_KV_WIDTH = A_KV_HEADS * HEAD_DIM
B_WIDTH = B_HEADS * HEAD_DIM
QK_SCALE = HEAD_DIM ** -0.5
LOG2E = 1.4426950408889634
V_ROWS = HEAD_DIM + 16
NEG = -1e30

R_QA = 0
R_KA = R_QA + A_WIDTH
R_VA = R_KA + A_KV_WIDTH
R_QB = R_VA + A_KV_WIDTH
R_KB = R_QB + B_WIDTH
R_VB = R_KB + B_WIDTH
R_GA = R_VB + B_WIDTH
R_GB = R_GA + D_MODEL
IN_COLS = R_GB + D_MODEL

NA_PAIR = 2 * GRID_W
NA_WIN_BLOCKS = 5
NA_WIN = NA_WIN_BLOCKS * NA_PAIR
NA_CLASSES = 5

VMEM_LIMIT = 56 * 1024 * 1024

TM = 512
SUB = 256
TQ = 512
TK = 1024
CB = 256
QG = 16

f32 = jnp.float32
bf16 = jnp.bfloat16


def _params(*sem):
    return pltpu.CompilerParams(dimension_semantics=sem, vmem_limit_bytes=VMEM_LIMIT)


def _full(shape):
    return pl.BlockSpec(shape, lambda *_: (0,) * len(shape))


def _head_norm_rope(y, gain, cos, sin):
    nh = y.shape[0] // HEAD_DIM
    y = y.reshape(nh, HEAD_DIM, y.shape[1])
    ms = jnp.mean(y * y, axis=1, keepdims=True)
    y = y * lax.rsqrt(ms + EPS) * gain
    q = HEAD_DIM // 4
    rot = jnp.concatenate([y[:, q:2 * q], y[:, 0:q], y[:, 3 * q:4 * q], y[:, 2 * q:3 * q]], axis=1)
    y = y * cos + rot * sin
    return y.reshape(nh * HEAD_DIM, y.shape[2])


def _in_proj_kernel(x_ref, g1_ref, w_ref, qg_ref, kg_ref, cos_ref, sin_ref, *outs):
    if len(outs) == 9:
        xt_ref, *outs = outs
        x_all = x_ref[...].T
        xt_ref[...] = x_all
    else:
        x_all = x_ref[...]
    qa_ref, ka_ref, va_ref, qb_ref, kb_ref, vb_ref, sga_ref, sgb_ref = outs
    nsub = x_all.shape[1] // SUB
    hs = []
    for t in range(nsub):
        x = x_all[:, t * SUB:(t + 1) * SUB]
        ms = jnp.mean(x * x, axis=0, keepdims=True)
        hs.append((x * lax.rsqrt(ms + EPS) * g1_ref[...]).astype(bf16))
    for t in range(nsub):
        tc = slice(t * SUB, (t + 1) * SUB)
        h = hs[t]
        cos = cos_ref[:, tc]
        sin = sin_ref[:, tc]

        def proj(r0, r1, h=h):
            return jnp.dot(w_ref[r0:r1, :], h, preferred_element_type=f32)

        qa = _head_norm_rope(proj(R_QA, R_KA), qg_ref[...], cos, sin)
        qa_ref[0, :, tc] = (qa * (QK_SCALE * LOG2E)).astype(bf16)
        ka = _head_norm_rope(proj(R_KA, R_VA), kg_ref[...], cos, sin)
        ka_ref[tc, :] = ka.T.astype(bf16)
        va = proj(R_VA, R_QB).astype(bf16)
        ones = jnp.ones((V_ROWS - HEAD_DIM, SUB), bf16)
        va_ref[0, :, tc] = jnp.concatenate([va[0:HEAD_DIM], ones, va[HEAD_DIM:2 * HEAD_DIM], ones], axis=0)
        qb = (proj(R_QB, R_KB) * (QK_SCALE * LOG2E)).astype(bf16)
        kb_ref[tc, :] = proj(R_KB, R_VB).T.astype(bf16)
        vb = proj(R_VB, R_GA).astype(bf16)
        for c in range(SUB // NA_PAIR):
            pc = slice(c * NA_PAIR, (c + 1) * NA_PAIR)
            qb_ref[t * (SUB // NA_PAIR) + c] = qb[:, pc]
            vb_ref[t * (SUB // NA_PAIR) + c] = vb[:, pc]
        sga_ref[:, tc] = jax.nn.sigmoid(proj(R_GA, R_GB)).astype(bf16)
        sgb_ref[:, tc] = jax.nn.sigmoid(proj(R_GB, IN_COLS)).astype(bf16)


def in_proj(x, g1, w_in_t, qg, kg, cos, sin, seq, token_major_in=False):
    t = x.shape[0] if token_major_in else x.shape[1]
    n_seq_tiles = seq // TM
    tab = pl.BlockSpec((HEAD_DIM, TM), lambda i: (0, i % n_seq_tiles))

    def fm(rows):
        return pl.BlockSpec((rows, TM), lambda i: (0, i))

    def tok(cols):
        return pl.BlockSpec((TM, cols), lambda i: (i, 0))

    pair_tiles = pl.BlockSpec((TM // NA_PAIR, B_WIDTH, NA_PAIR), lambda i: (i, 0, 0))
    out_shapes = (
        jax.ShapeDtypeStruct((t // TM, A_WIDTH, TM), bf16),
        jax.ShapeDtypeStruct((t, A_KV_WIDTH), bf16),
        jax.ShapeDtypeStruct((t // TK, A_KV_HEADS * V_ROWS, TK), bf16),
        jax.ShapeDtypeStruct((t // NA_PAIR, B_WIDTH, NA_PAIR), bf16),
        jax.ShapeDtypeStruct((t, B_WIDTH), bf16),
        jax.ShapeDtypeStruct((t // NA_PAIR, B_WIDTH, NA_PAIR), bf16),
        jax.ShapeDtypeStruct((D_MODEL, t), bf16),
        jax.ShapeDtypeStruct((D_MODEL, t), bf16),
    )
    out_specs = (
        pl.BlockSpec((1, A_WIDTH, TM), lambda i: (i, 0, 0)), tok(A_KV_WIDTH),
        pl.BlockSpec((1, A_KV_HEADS * V_ROWS, TM), lambda i: (i // (TK // TM), 0, i % (TK // TM))),
        pair_tiles, tok(B_WIDTH), pair_tiles, fm(D_MODEL), fm(D_MODEL),
    )
    if token_major_in:
        out_shapes = (jax.ShapeDtypeStruct((D_MODEL, t), f32),) + out_shapes
        out_specs = (fm(D_MODEL),) + out_specs
    return pl.pallas_call(
        _in_proj_kernel,
        grid=(t // TM,),
        in_specs=[tok(D_MODEL) if token_major_in else fm(D_MODEL), _full((D_MODEL, 1)), _full((IN_COLS, D_MODEL)),
                  _full((HEAD_DIM, 1)), _full((HEAD_DIM, 1)), tab, tab],
        out_specs=out_specs,
        out_shape=out_shapes,
        compiler_params=_params("parallel"),
        name="in_proj",
    )(x, g1.reshape(D_MODEL, 1), w_in_t, qg.reshape(HEAD_DIM, 1), kg.reshape(HEAD_DIM, 1), cos, sin)


def _global_attn_kernel(q_ref, k_ref, v_ref, o_ref, s_scr):
    n_q, _, tq = q_ref.shape
    n_kv = v_ref.shape[0]
    tk = v_ref.shape[2]
    n = A_GROUPS * tq
    j = pl.program_id(1)
    ncb = n // CB
    cols = [slice(c * CB, (c + 1) * CB) for c in range(ncb)]

    def load_q(qi):
        qt = q_ref[qi]
        q = jnp.concatenate([qt[g * HEAD_DIM:(g + 1) * HEAD_DIM, :] for g in range(A_GROUPS)], axis=1)
        zeros = jnp.zeros_like(q)
        qpad = jnp.concatenate([jnp.where(j == 0, q, zeros), jnp.where(j == 1, q, zeros)], axis=0)
        return [qpad[:, c] for c in cols]

    def produce(i, slot, c, qblk):
        k = k_ref[pl.ds(pl.multiple_of(i * tk, tk), tk), :]
        s = jnp.dot(k, qblk[c], preferred_element_type=f32)
        s_scr[slot, :, cols[c]] = s
        return jnp.max(s, axis=0, keepdims=True)

    def consume(i, slot, c, mt, m, l, acc):
        m_new = jnp.maximum(m, mt)
        alpha = jnp.exp2(m - m_new)
        p = jnp.exp2(s_scr[slot, :, cols[c]] - m_new).astype(bf16)
        pv = jnp.dot(v_ref[i], p, preferred_element_type=f32)
        l = alpha * l + pv[HEAD_DIM:HEAD_DIM + 1, :]
        acc = alpha * acc + pv[0:HEAD_DIM, :]
        return m_new, l, acc

    def half(i_cur, slot_cur, state, qblk, i_next):
        out = []
        for c in range(ncb):
            mt, m, l, acc = state[c]
            mt_next = produce(i_next, 1 - slot_cur, c, qblk)
            out.append((mt_next,) + consume(i_cur, slot_cur, c, mt, m, l, acc))
        return out

    def q_tile(qi, mts):
        qblk = load_q(qi)
        state = [(mts[c], jnp.full((1, CB), NEG, f32), jnp.zeros((1, CB), f32), jnp.zeros((HEAD_DIM, CB), f32))
                 for c in range(ncb)]

        def body(it, state):
            i = 2 * it
            return half(i + 1, 1, half(i, 0, state, qblk, i + 1), qblk, i + 2)

        state = lax.fori_loop(0, n_kv // 2 - 1, body, state)
        state = half(n_kv - 2, 0, state, qblk, n_kv - 1)
        state = half(n_kv - 1, 1, state, load_q(jnp.minimum(qi + 1, n_q - 1)), 0)
        out = jnp.concatenate([(acc / l).astype(bf16) for _, _, l, acc in state], axis=1)
        o_ref[qi] = jnp.concatenate([out[:, g * tq:(g + 1) * tq] for g in range(A_GROUPS)], axis=0)
        return [st[0] for st in state]

    qblk0 = load_q(0)
    lax.fori_loop(0, n_q, q_tile, [produce(0, 0, c, qblk0) for c in range(ncb)])


def global_attn(qa, ka, va, batch, seq):
    n_g = seq // (QG * TQ)
    n_kv = seq // TK
    qo_spec = pl.BlockSpec((QG, A_GROUPS * HEAD_DIM, TQ), lambda b, j, g: (b * n_g + g, j, 0))
    return pl.pallas_call(
        _global_attn_kernel,
        grid=(batch, A_KV_HEADS, n_g),
        in_specs=[qo_spec,
                  pl.BlockSpec((seq, A_KV_WIDTH), lambda b, j, g: (b, 0)),
                  pl.BlockSpec((n_kv, V_ROWS, TK), lambda b, j, g: (b, j, 0))],
        out_specs=qo_spec,
        out_shape=jax.ShapeDtypeStruct(qa.shape, bf16),
        scratch_shapes=[pltpu.VMEM((2, TK, A_GROUPS * TQ), f32)],
        compiler_params=_params("parallel", "parallel", "parallel"),
        name="global_attn",
    )(qa, ka, va)


def _na_bias_index():
    x = np.arange(GRID_W)[:, None]
    c = np.arange(GRID_W)[None, :]
    cs = np.clip(c - NA_COLS // 2, 0, GRID_W - NA_COLS)
    col_ok = (x >= cs) & (x < cs + NA_COLS)
    dc = np.clip(x - c, -(NA_COLS - 1), NA_COLS - 1) + (NA_COLS - 1)
    onehot = (dc[None] == np.arange(2 * NA_COLS - 1)[:, None, None]).astype(np.float32)
    cls = [(0, (0, 0)), (2, (0, 0)), (4, (0, 1)), (6, (2, 2)), (8, (2, 2))]
    i = np.arange(NA_WIN_BLOCKS * 2)[:, None]
    u = np.arange(2)[None, :]
    rel, row_ok = [], []
    for delta, rs_off in cls:
        off = np.asarray(rs_off)[None, :]
        rel.append(np.clip(i - delta - u + (NA_ROWS - 1), 0, 2 * NA_ROWS - 2))
        row_ok.append((i - off >= 0) & (i - off < NA_ROWS))
    return onehot, col_ok, np.stack(rel).astype(np.int32), np.stack(row_ok)


def na_bias_tiles(rpb):
    onehot, col_ok, rel, row_ok = _na_bias_index()
    depth = rpb.shape[0]
    t1 = jnp.einsum("lhad,dxc->lhaxc", rpb.astype(f32), jnp.asarray(onehot), precision=lax.Precision.HIGHEST)
    parts = []
    for e in range(2):
        for u in range(2):
            g = jnp.take(t1[:, e::2], jnp.asarray(rel[:, :, u].reshape(-1)), axis=2)
            ok = (row_ok[:, :, u, None, None] & col_ok[None, None]).reshape(-1, GRID_W, GRID_W)
            parts.append(jnp.where(jnp.asarray(ok)[None, None], g, NEG))
    g = jnp.concatenate(parts, axis=-1)
    return g.reshape(depth, B_HEADS // 2, NA_CLASSES, NA_WIN, 2 * NA_PAIR) * LOG2E


NA_UNROLL = 8


def _nbr_attn_kernel(q_ref, k_ref, v_ref, b_ref, o_ref, s_scr, p_scr):
    n_pairs = q_ref.shape[0]
    zeros = jnp.zeros((HEAD_DIM, NA_PAIR), bf16)
    ones = jnp.ones((16, NA_WIN), bf16)

    def win(p):
        return jnp.clip(p - 2, 0, n_pairs - NA_WIN_BLOCKS)

    def cls(p):
        return jnp.where(p < 2, p, jnp.where(p >= n_pairs - 2, p - (n_pairs - 2) + 3, 2))

    def scores(p, slot):
        q2 = q_ref[p]
        qbd = jnp.concatenate([jnp.concatenate([q2[0:HEAD_DIM], zeros], axis=1),
                               jnp.concatenate([zeros, q2[HEAD_DIM:]], axis=1)], axis=0)
        w, c = win(p), cls(p)
        mx = None
        for r in range(NA_WIN_BLOCKS):
            rr = slice(r * NA_PAIR, (r + 1) * NA_PAIR)
            k = k_ref[pl.ds(pl.multiple_of((w + r) * NA_PAIR, NA_PAIR), NA_PAIR), :]
            x = jnp.dot(k, qbd, preferred_element_type=f32) + b_ref[0, c, rr, :]
            s_scr[slot, rr, :] = x
            xm = jnp.max(x, axis=0, keepdims=True)
            mx = xm if mx is None else jnp.maximum(mx, xm)
        return mx

    def probs(slot, m):
        p_scr[slot] = jnp.exp2(s_scr[slot] - m).astype(bf16)

    def weighted(p, slot):
        w = win(p)
        vwin = jnp.concatenate([v_ref[w + r] for r in range(NA_WIN_BLOCKS)], axis=1)
        vwin = jnp.concatenate([vwin, ones], axis=0)
        pv = jnp.dot(vwin, p_scr[slot], preferred_element_type=f32)
        o = pv[0:2 * HEAD_DIM] / pv[2 * HEAD_DIM:2 * HEAD_DIM + 1]
        o_ref[p] = jnp.concatenate([o[0:HEAD_DIM, 0:NA_PAIR], o[HEAD_DIM:, NA_PAIR:]], axis=0).astype(bf16)

    def step(p, slot, m_next):
        m_new = scores(jnp.minimum(p + 2, n_pairs - 1), slot)
        probs(1 - slot, m_next)
        weighted(p, slot)
        return m_new

    m0 = scores(0, 0)
    m1 = scores(1, 1)
    probs(0, m0)

    def body(it, m_next):
        for r in range(NA_UNROLL):
            m_next = step(NA_UNROLL * it + r, r % 2, m_next)
        return m_next

    lax.fori_loop(0, n_pairs // NA_UNROLL, body, m1)


def nbr_attn(qb, kb, vb, bias, batch, seq):
    n_pairs = seq // NA_PAIR
    tile = pl.BlockSpec((n_pairs, 2 * HEAD_DIM, NA_PAIR), lambda b, hp: (b, hp, 0))
    return pl.pallas_call(
        _nbr_attn_kernel,
        grid=(batch, B_HEADS // 2),
        in_specs=[tile, pl.BlockSpec((seq, 2 * HEAD_DIM), lambda b, hp: (b, hp)), tile,
                  pl.BlockSpec((1, NA_CLASSES, NA_WIN, 2 * NA_PAIR), lambda b, hp: (hp, 0, 0, 0))],
        out_specs=tile,
        out_shape=jax.ShapeDtypeStruct(qb.shape, bf16),
        scratch_shapes=[pltpu.VMEM((2, NA_WIN, 2 * NA_PAIR), f32), pltpu.VMEM((2, NA_WIN, 2 * NA_PAIR), bf16)],
        compiler_params=_params("parallel", "parallel"),
        name="nbr_attn",
    )(qb, kb, vb, bias)


def _resident(shape):
    return pl.BlockSpec(shape, lambda *_: (0,) * len(shape), pipeline_mode=pl.Buffered(1))


FF_CHUNKS = ((0, 1024), (1024, 2048), (2048, D_FF))


def _mix_ffn_kernel(x_ref, ya_ref, yb_ref, sga_ref, sgb_ref, wa_ref, wb_ref, wo_ref, g2_ref, wgu_ref, wd_ref, *rest):
    o_ref = rest[-1]
    ua = jnp.dot(wa_ref[...], ya_ref[0], preferred_element_type=f32)
    yb = jnp.concatenate([yb_ref[c] for c in range(yb_ref.shape[0])], axis=1)
    ub = jnp.dot(wb_ref[...], yb, preferred_element_type=f32)
    mix = sga_ref[...].astype(f32) * ua + sgb_ref[...].astype(f32) * ub
    x = x_ref[...] + jnp.dot(wo_ref[...], mix.astype(bf16), preferred_element_type=f32)
    ms = jnp.mean(x * x, axis=0, keepdims=True)
    h = (x * lax.rsqrt(ms + EPS) * g2_ref[...]).astype(bf16)
    acc = x
    for c0, c1 in FF_CHUNKS:
        gate = jnp.dot(wgu_ref[c0:c1, :], h, preferred_element_type=f32)
        up = jnp.dot(wgu_ref[D_FF + c0:D_FF + c1, :], h, preferred_element_type=f32)
        act = (jax.nn.silu(gate) * up).astype(bf16)
        acc = acc + jnp.dot(wd_ref[:, c0:c1], act, preferred_element_type=f32)
    if len(rest) == 1:
        o_ref[...] = acc
    else:
        ms = jnp.mean(acc * acc, axis=0, keepdims=True)
        o_ref[...] = (acc * lax.rsqrt(ms + EPS) * rest[0][...]).T


def mix_ffn(xt, ya, yb, sga, sgb, wa_t, wb_t, wo_t, g2, wgu_t, wd_t, final_g=None):
    t = xt.shape[1]
    last = final_g is not None

    def fm(rows):
        return pl.BlockSpec((rows, TM), lambda i: (0, i))

    return pl.pallas_call(
        _mix_ffn_kernel,
        grid=(t // TM,),
        in_specs=[fm(D_MODEL), pl.BlockSpec((1, A_WIDTH, TM), lambda i: (i, 0, 0)),
                  pl.BlockSpec((TM // NA_PAIR, B_WIDTH, NA_PAIR), lambda i: (i, 0, 0)), fm(D_MODEL), fm(D_MODEL),
                  _resident((D_MODEL, A_WIDTH)), _resident((D_MODEL, B_WIDTH)), _resident((D_MODEL, D_MODEL)),
                  _resident((D_MODEL, 1)), _resident((2 * D_FF, D_MODEL)), _resident((D_MODEL, D_FF))]
        + ([_resident((D_MODEL, 1))] if last else []),
        out_specs=pl.BlockSpec((TM, D_MODEL), lambda i: (i, 0)) if last else fm(D_MODEL),
        out_shape=jax.ShapeDtypeStruct((t, D_MODEL) if last else (D_MODEL, t), f32),
        compiler_params=_params("parallel"),
        name="mix_ffn",
    )(xt, ya, yb, sga, sgb, wa_t, wb_t, wo_t, g2.reshape(D_MODEL, 1), wgu_t, wd_t,
      *([final_g.reshape(D_MODEL, 1)] if last else []))


W_COLS = 256


def _weight_t_kernel(w_ref, o_ref):
    o_ref[0] = w_ref[0].T.astype(bf16)


def weight_t(w):
    depth, n_in, n_out = w.shape
    return pl.pallas_call(
        _weight_t_kernel,
        grid=(depth, n_out // W_COLS),
        in_specs=[pl.BlockSpec((1, n_in, W_COLS), lambda l, j: (l, 0, j))],
        out_specs=pl.BlockSpec((1, W_COLS, n_in), lambda l, j: (l, j, 0)),
        out_shape=jax.ShapeDtypeStruct((depth, n_out, n_in), bf16),
        compiler_params=_params("parallel", "parallel"),
        name="weight_t",
    )(w)


def _rope_tables(seq):
    t = jnp.arange(seq, dtype=jnp.int32)
    row = (t // GRID_W).astype(f32)
    col = (t % GRID_W).astype(f32)
    quarter = HEAD_DIM // 4
    inv = ROPE_THETA ** (-jnp.arange(quarter, dtype=f32) / quarter)
    ang_r = inv[:, None] * row[None, :]
    ang_c = inv[:, None] * col[None, :]
    cos = jnp.concatenate([jnp.cos(ang_r)] * 2 + [jnp.cos(ang_c)] * 2, axis=0)
    sin = jnp.concatenate([-jnp.sin(ang_r), jnp.sin(ang_r), -jnp.sin(ang_c), jnp.sin(ang_c)], axis=0)
    return cos, sin


def _trunk(x, weights, final_g):
    batch, seq, _ = x.shape
    assert TM == TQ and seq % max(QG * TQ, TK) == 0 and seq % (NA_UNROLL * NA_PAIR) == 0 and seq // NA_PAIR >= NA_WIN_BLOCKS
    cos, sin = _rope_tables(seq)
    xt = x.reshape(batch * seq, D_MODEL)
    for layer, w in enumerate(weights):
        outs = in_proj(xt, w["norm1"], w["w_in_t"], w["q_norm"], w["k_norm"], cos, sin, seq,
                       token_major_in=layer == 0)
        if layer == 0:
            xt, *outs = outs
        qa, ka, va, qb, kb, vb, sga, sgb = outs
        ya = global_attn(qa, ka, va, batch, seq)
        yb = nbr_attn(qb, kb, vb, w["na_bias"], batch, seq)
        xt = mix_ffn(xt, ya, yb, sga, sgb, w["w_up_a_t"], w["w_up_b_t"], w["w_out_t"],
                     w["norm2"], w["w_gate_up_t"], w["w_down_t"],
                     final_g if layer == len(weights) - 1 else None)
    return xt.reshape(batch, seq, D_MODEL)


def kernel(x_prompt, x_sample, norm1, w_in, q_norm, k_norm, rpb, w_up_a, w_up_b, w_out, norm2, w_gate_up,
           w_down, final_norm):
    depth = norm1.shape[0]

    w_in_t, w_up_a_t, w_up_b_t, w_out_t, w_gate_up_t, w_down_t = (
        weight_t(w) for w in (w_in, w_up_a, w_up_b, w_out, w_gate_up, w_down))
    na_bias = na_bias_tiles(rpb)
    weights = [dict(norm1=norm1[l], w_in_t=w_in_t[l], q_norm=q_norm[l], k_norm=k_norm[l], na_bias=na_bias[l],
                    w_up_a_t=w_up_a_t[l], w_up_b_t=w_up_b_t[l], w_out_t=w_out_t[l], norm2=norm2[l],
                    w_gate_up_t=w_gate_up_t[l], w_down_t=w_down_t[l]) for l in range(depth)]
    return (_trunk(x_prompt, weights, final_norm), _trunk(x_sample, weights, final_norm))
```

```python
import numpy as np
import jax
import jax.numpy as jnp
from jax import lax
from jax.experimental import pallas as pl
from jax.experimental.pallas import tpu as pltpu

D_MODEL = 1024
GRID_W = 64
HEAD_DIM = 64
A_Q_HEADS = 8
A_KV_HEADS = 2
A_GROUPS = A_Q_HEADS // A_KV_HEADS
B_HEADS = 8
NA_ROWS = 8
NA_COLS = 16
ROPE_THETA = 10000.0
EPS = 1e-6
D_FF = -(-8 * D_MODEL // (3 * 256)) * 256
A_WIDTH = A_Q_HEADS * HEAD_DIM
A_KV_WIDTH = A_KV_HEADS * HEAD_DIM
B_WIDTH = B_HEADS * HEAD_DIM
QK_SCALE = HEAD_DIM ** -0.5
LOG2E = 1.4426950408889634
V_ROWS = HEAD_DIM + 16
NEG = -1e30

R_QA = 0
R_KA = R_QA + A_WIDTH
R_VA = R_KA + A_KV_WIDTH
R_QB = R_VA + A_KV_WIDTH
R_KB = R_QB + B_WIDTH
R_VB = R_KB + B_WIDTH
R_GA = R_VB + B_WIDTH
R_GB = R_GA + D_MODEL
IN_COLS = R_GB + D_MODEL

NA_PAIR = 2 * GRID_W
NA_WIN_BLOCKS = 5
NA_WIN = NA_WIN_BLOCKS * NA_PAIR
NA_CLASSES = 5

VMEM_LIMIT = 56 * 1024 * 1024

TM = 512
SUB = 256
TQ = 512
TK = 1024
CB = 256
QG = 16

f32 = jnp.float32
bf16 = jnp.bfloat16


def _params(*sem):
    return pltpu.CompilerParams(dimension_semantics=sem, vmem_limit_bytes=VMEM_LIMIT)


def _full(shape):
    return pl.BlockSpec(shape, lambda *_: (0,) * len(shape))


def _of_layer(shape, layer, **kw):
    return pl.BlockSpec((None,) + tuple(shape), lambda *_: (layer,) + (0,) * len(shape), **kw)


def _head_norm_rope(y, gain, cos, sin):
    nh = y.shape[0] // HEAD_DIM
    y = y.reshape(nh, HEAD_DIM, y.shape[1])
    ms = jnp.mean(y * y, axis=1, keepdims=True)
    y = y * lax.rsqrt(ms + EPS) * gain
    q = HEAD_DIM // 4
    rot = jnp.concatenate([y[:, q:2 * q], y[:, 0:q], y[:, 3 * q:4 * q], y[:, 2 * q:3 * q]], axis=1)
    y = y * cos + rot * sin
    return y.reshape(nh * HEAD_DIM, y.shape[2])


def _in_proj_kernel(x_ref, g1_ref, w_ref, qg_ref, kg_ref, cos_ref, sin_ref, *outs):
    if len(outs) == 9:
        xt_ref, *outs = outs
        x_all = x_ref[...].T
        xt_ref[...] = x_all
    else:
        x_all = x_ref[...]
    qa_ref, ka_ref, va_ref, qb_ref, kb_ref, vb_ref, sga_ref, sgb_ref = outs
    nsub = x_all.shape[1] // SUB
    hs = []
    for t in range(nsub):
        x = x_all[:, t * SUB:(t + 1) * SUB]
        ms = jnp.mean(x * x, axis=0, keepdims=True)
        hs.append((x * lax.rsqrt(ms + EPS) * g1_ref[...]).astype(bf16))
    for t in range(nsub):
        tc = slice(t * SUB, (t + 1) * SUB)
        h = hs[t]
        cos = cos_ref[:, tc]
        sin = sin_ref[:, tc]

        def proj(r0, r1, h=h):
            return jnp.dot(w_ref[r0:r1, :], h, preferred_element_type=f32)

        qa = _head_norm_rope(proj(R_QA, R_KA), qg_ref[...], cos, sin)
        qa_ref[0, :, tc] = (qa * (QK_SCALE * LOG2E)).astype(bf16)
        ka = _head_norm_rope(proj(R_KA, R_VA), kg_ref[...], cos, sin)
        ka_ref[tc, :] = ka.T.astype(bf16)
        va = proj(R_VA, R_QB).astype(bf16)
        ones = jnp.ones((V_ROWS - HEAD_DIM, SUB), bf16)
        va_ref[0, :, tc] = jnp.concatenate([va[0:HEAD_DIM], ones, va[HEAD_DIM:2 * HEAD_DIM], ones], axis=0)
        qb = (proj(R_QB, R_KB) * (QK_SCALE * LOG2E)).astype(bf16)
        kb_ref[tc, :] = proj(R_KB, R_VB).T.astype(bf16)
        vb = proj(R_VB, R_GA).astype(bf16)
        for c in range(SUB // NA_PAIR):
            pc = slice(c * NA_PAIR, (c + 1) * NA_PAIR)
            qb_ref[t * (SUB // NA_PAIR) + c] = qb[:, pc]
            vb_ref[t * (SUB // NA_PAIR) + c] = vb[:, pc]
        sga_ref[:, tc] = jax.nn.sigmoid(proj(R_GA, R_GB)).astype(bf16)
        sgb_ref[:, tc] = jax.nn.sigmoid(proj(R_GB, IN_COLS)).astype(bf16)


def in_proj(x, g1, w_in_t, layer, qg, kg, cos, sin, seq, token_major_in=False):
    t = x.shape[0] if token_major_in else x.shape[1]
    n_seq_tiles = seq // TM
    tab = pl.BlockSpec((HEAD_DIM, TM), lambda i: (0, i % n_seq_tiles))

    def fm(rows):
        return pl.BlockSpec((rows, TM), lambda i: (0, i))

    def tok(cols):
        return pl.BlockSpec((TM, cols), lambda i: (i, 0))

    pair_tiles = pl.BlockSpec((TM // NA_PAIR, B_WIDTH, NA_PAIR), lambda i: (i, 0, 0))
    out_shapes = (
        jax.ShapeDtypeStruct((t // TM, A_WIDTH, TM), bf16),
        jax.ShapeDtypeStruct((t, A_KV_WIDTH), bf16),
        jax.ShapeDtypeStruct((t // TK, A_KV_HEADS * V_ROWS, TK), bf16),
        jax.ShapeDtypeStruct((t // NA_PAIR, B_WIDTH, NA_PAIR), bf16),
        jax.ShapeDtypeStruct((t, B_WIDTH), bf16),
        jax.ShapeDtypeStruct((t // NA_PAIR, B_WIDTH, NA_PAIR), bf16),
        jax.ShapeDtypeStruct((D_MODEL, t), bf16),
        jax.ShapeDtypeStruct((D_MODEL, t), bf16),
    )
    out_specs = (
        pl.BlockSpec((1, A_WIDTH, TM), lambda i: (i, 0, 0)), tok(A_KV_WIDTH),
        pl.BlockSpec((1, A_KV_HEADS * V_ROWS, TM), lambda i: (i // (TK // TM), 0, i % (TK // TM))),
        pair_tiles, tok(B_WIDTH), pair_tiles, fm(D_MODEL), fm(D_MODEL),
    )
    if token_major_in:
        out_shapes = (jax.ShapeDtypeStruct((D_MODEL, t), f32),) + out_shapes
        out_specs = (fm(D_MODEL),) + out_specs
    return pl.pallas_call(
        _in_proj_kernel,
        grid=(t // TM,),
        in_specs=[tok(D_MODEL) if token_major_in else fm(D_MODEL), _full((D_MODEL, 1)), _of_layer((IN_COLS, D_MODEL), layer),
                  _full((HEAD_DIM, 1)), _full((HEAD_DIM, 1)), tab, tab],
        out_specs=out_specs,
        out_shape=out_shapes,
        compiler_params=_params("parallel"),
        name="in_proj",
    )(x, g1.reshape(D_MODEL, 1), w_in_t, qg.reshape(HEAD_DIM, 1), kg.reshape(HEAD_DIM, 1), cos, sin)


def _global_attn_kernel(q_ref, k_ref, v_ref, o_ref, s_scr):
    n_q, _, tq = q_ref.shape
    n_kv = v_ref.shape[0]
    tk = v_ref.shape[2]
    n = A_GROUPS * tq
    j = pl.program_id(1)
    ncb = n // CB
    cols = [slice(c * CB, (c + 1) * CB) for c in range(ncb)]

    def load_q(qi):
        qt = q_ref[qi]
        q = jnp.concatenate([qt[g * HEAD_DIM:(g + 1) * HEAD_DIM, :] for g in range(A_GROUPS)], axis=1)
        zeros = jnp.zeros_like(q)
        qpad = jnp.concatenate([jnp.where(j == 0, q, zeros), jnp.where(j == 1, q, zeros)], axis=0)
        return [qpad[:, c] for c in cols]

    def produce(i, slot, c, qblk):
        k = k_ref[pl.ds(pl.multiple_of(i * tk, tk), tk), :]
        s = jnp.dot(k, qblk[c], preferred_element_type=f32)
        s_scr[slot, :, cols[c]] = s
        return jnp.max(s, axis=0, keepdims=True)

    def consume(i, slot, c, mt, m, l, acc):
        m_new = jnp.maximum(m, mt)
        alpha = jnp.exp2(m - m_new)
        p = jnp.exp2(s_scr[slot, :, cols[c]] - m_new).astype(bf16)
        pv = jnp.dot(v_ref[i], p, preferred_element_type=f32)
        l = alpha * l + pv[HEAD_DIM:HEAD_DIM + 1, :]
        acc = alpha * acc + pv[0:HEAD_DIM, :]
        return m_new, l, acc

    def half(i_cur, slot_cur, state, qblk, i_next):
        out = []
        for c in range(ncb):
            mt, m, l, acc = state[c]
            mt_next = produce(i_next, 1 - slot_cur, c, qblk)
            out.append((mt_next,) + consume(i_cur, slot_cur, c, mt, m, l, acc))
        return out

    def q_tile(qi, mts):
        qblk = load_q(qi)
        state = [(mts[c], jnp.full((1, CB), NEG, f32), jnp.zeros((1, CB), f32), jnp.zeros((HEAD_DIM, CB), f32))
                 for c in range(ncb)]

        def body(it, state):
            i = 2 * it
            return half(i + 1, 1, half(i, 0, state, qblk, i + 1), qblk, i + 2)

        state = lax.fori_loop(0, n_kv // 2 - 1, body, state)
        state = half(n_kv - 2, 0, state, qblk, n_kv - 1)
        state = half(n_kv - 1, 1, state, load_q(jnp.minimum(qi + 1, n_q - 1)), 0)
        out = jnp.concatenate([(acc / l).astype(bf16) for _, _, l, acc in state], axis=1)
        o_ref[qi] = jnp.concatenate([out[:, g * tq:(g + 1) * tq] for g in range(A_GROUPS)], axis=0)
        return [st[0] for st in state]

    qblk0 = load_q(0)
    lax.fori_loop(0, n_q, q_tile, [produce(0, 0, c, qblk0) for c in range(ncb)])


def global_attn(qa, ka, va, batch, seq):
    n_g = seq // (QG * TQ)
    n_kv = seq // TK
    qo_spec = pl.BlockSpec((QG, A_GROUPS * HEAD_DIM, TQ), lambda b, j, g: (b * n_g + g, j, 0))
    return pl.pallas_call(
        _global_attn_kernel,
        grid=(batch, A_KV_HEADS, n_g),
        in_specs=[qo_spec,
                  pl.BlockSpec((seq, A_KV_WIDTH), lambda b, j, g: (b, 0)),
                  pl.BlockSpec((n_kv, V_ROWS, TK), lambda b, j, g: (b, j, 0))],
        out_specs=qo_spec,
        out_shape=jax.ShapeDtypeStruct(qa.shape, bf16),
        scratch_shapes=[pltpu.VMEM((2, TK, A_GROUPS * TQ), f32)],
        compiler_params=_params("parallel", "parallel", "parallel"),
        name="global_attn",
    )(qa, ka, va)


def _na_bias_index():
    x = np.arange(GRID_W)[:, None]
    c = np.arange(GRID_W)[None, :]
    cs = np.clip(c - NA_COLS // 2, 0, GRID_W - NA_COLS)
    col_ok = (x >= cs) & (x < cs + NA_COLS)
    dc = np.clip(x - c, -(NA_COLS - 1), NA_COLS - 1) + (NA_COLS - 1)
    onehot = (dc[None] == np.arange(2 * NA_COLS - 1)[:, None, None]).astype(np.float32)
    cls = [(0, (0, 0)), (2, (0, 0)), (4, (0, 1)), (6, (2, 2)), (8, (2, 2))]
    i = np.arange(NA_WIN_BLOCKS * 2)[:, None]
    u = np.arange(2)[None, :]
    rel, row_ok = [], []
    for delta, rs_off in cls:
        off = np.asarray(rs_off)[None, :]
        rel.append(np.clip(i - delta - u + (NA_ROWS - 1), 0, 2 * NA_ROWS - 2))
        row_ok.append((i - off >= 0) & (i - off < NA_ROWS))
    return onehot, col_ok, np.stack(rel).astype(np.int32), np.stack(row_ok)


def na_bias_tiles(rpb):
    onehot, col_ok, rel, row_ok = _na_bias_index()
    depth = rpb.shape[0]
    t1 = jnp.einsum("lhad,dxc->lhaxc", rpb.astype(f32), jnp.asarray(onehot), precision=lax.Precision.HIGHEST)
    parts = []
    for e in range(2):
        for u in range(2):
            g = jnp.take(t1[:, e::2], jnp.asarray(rel[:, :, u].reshape(-1)), axis=2)
            ok = (row_ok[:, :, u, None, None] & col_ok[None, None]).reshape(-1, GRID_W, GRID_W)
            parts.append(jnp.where(jnp.asarray(ok)[None, None], g, NEG))
    g = jnp.concatenate(parts, axis=-1)
    return g.reshape(depth, B_HEADS // 2, NA_CLASSES, NA_WIN, 2 * NA_PAIR) * LOG2E


NA_UNROLL = 16


def _nbr_attn_kernel(q_ref, k_ref, v_ref, b_ref, o_ref, s_scr, p_scr):
    n_pairs = q_ref.shape[0]
    zeros = jnp.zeros((HEAD_DIM, NA_PAIR), bf16)
    ones = jnp.ones((16, NA_WIN), bf16)

    def win(p):
        return jnp.clip(p - 2, 0, n_pairs - NA_WIN_BLOCKS)

    def cls(p):
        return jnp.where(p < 2, p, jnp.where(p >= n_pairs - 2, p - (n_pairs - 2) + 3, 2))

    def scores(p, slot):
        q2 = q_ref[p]
        qbd = jnp.concatenate([jnp.concatenate([q2[0:HEAD_DIM], zeros], axis=1),
                               jnp.concatenate([zeros, q2[HEAD_DIM:]], axis=1)], axis=0)
        w, c = win(p), cls(p)
        mx = None
        for r in range(NA_WIN_BLOCKS):
            rr = slice(r * NA_PAIR, (r + 1) * NA_PAIR)
            k = k_ref[pl.ds(pl.multiple_of((w + r) * NA_PAIR, NA_PAIR), NA_PAIR), :]
            x = jnp.dot(k, qbd, preferred_element_type=f32) + b_ref[0, c, rr, :]
            s_scr[slot, rr, :] = x
            xm = jnp.max(x, axis=0, keepdims=True)
            mx = xm if mx is None else jnp.maximum(mx, xm)
        return mx

    def probs(slot, m):
        p_scr[slot] = jnp.exp2(s_scr[slot] - m).astype(bf16)

    def weighted(p, slot):
        w = win(p)
        vwin = jnp.concatenate([v_ref[w + r] for r in range(NA_WIN_BLOCKS)], axis=1)
        vwin = jnp.concatenate([vwin, ones], axis=0)
        pv = jnp.dot(vwin, p_scr[slot], preferred_element_type=f32)
        o = pv[0:2 * HEAD_DIM] / pv[2 * HEAD_DIM:2 * HEAD_DIM + 1]
        o_ref[p] = jnp.concatenate([o[0:HEAD_DIM, 0:NA_PAIR], o[HEAD_DIM:, NA_PAIR:]], axis=0).astype(bf16)

    def step(p, slot, m_next):
        m_new = scores(jnp.minimum(p + 2, n_pairs - 1), slot)
        probs(1 - slot, m_next)
        weighted(p, slot)
        return m_new

    m0 = scores(0, 0)
    m1 = scores(1, 1)
    probs(0, m0)

    def body(it, m_next):
        for r in range(NA_UNROLL):
            m_next = step(NA_UNROLL * it + r, r % 2, m_next)
        return m_next

    lax.fori_loop(0, n_pairs // NA_UNROLL, body, m1)


def nbr_attn(qb, kb, vb, bias, layer, batch, seq):
    n_pairs = seq // NA_PAIR
    tile = pl.BlockSpec((n_pairs, 2 * HEAD_DIM, NA_PAIR), lambda b, hp: (b, hp, 0))
    return pl.pallas_call(
        _nbr_attn_kernel,
        grid=(batch, B_HEADS // 2),
        in_specs=[tile, pl.BlockSpec((seq, 2 * HEAD_DIM), lambda b, hp: (b, hp)), tile,
                  pl.BlockSpec((None, 1, NA_CLASSES, NA_WIN, 2 * NA_PAIR), lambda b, hp: (layer, hp, 0, 0, 0))],
        out_specs=tile,
        out_shape=jax.ShapeDtypeStruct(qb.shape, bf16),
        scratch_shapes=[pltpu.VMEM((2, NA_WIN, 2 * NA_PAIR), f32), pltpu.VMEM((2, NA_WIN, 2 * NA_PAIR), bf16)],
        compiler_params=_params("parallel", "parallel"),
        name="nbr_attn",
    )(qb, kb, vb, bias)


def _resident(shape):
    return pl.BlockSpec(shape, lambda *_: (0,) * len(shape), pipeline_mode=pl.Buffered(1))


FF_CHUNKS = ((0, 1024), (1024, 2048), (2048, D_FF))


def _mix_ffn_kernel(x_ref, ya_ref, yb_ref, sga_ref, sgb_ref, wa_ref, wb_ref, wo_ref, g2_ref, wgu_ref, wd_ref, *rest):
    o_ref = rest[-1]
    ua = jnp.dot(wa_ref[...], ya_ref[0], preferred_element_type=f32)
    yb = jnp.concatenate([yb_ref[c] for c in range(yb_ref.shape[0])], axis=1)
    ub = jnp.dot(wb_ref[...], yb, preferred_element_type=f32)
    mix = sga_ref[...].astype(f32) * ua + sgb_ref[...].astype(f32) * ub
    x = x_ref[...] + jnp.dot(wo_ref[...], mix.astype(bf16), preferred_element_type=f32)
    ms = jnp.mean(x * x, axis=0, keepdims=True)
    h = (x * lax.rsqrt(ms + EPS) * g2_ref[...]).astype(bf16)
    acc = x
    for c0, c1 in FF_CHUNKS:
        gate = jnp.dot(wgu_ref[c0:c1, :], h, preferred_element_type=f32)
        up = jnp.dot(wgu_ref[D_FF + c0:D_FF + c1, :], h, preferred_element_type=f32)
        act = (jax.nn.silu(gate) * up).astype(bf16)
        acc = acc + jnp.dot(wd_ref[:, c0:c1], act, preferred_element_type=f32)
    if len(rest) == 1:
        o_ref[...] = acc
    else:
        ms = jnp.mean(acc * acc, axis=0, keepdims=True)
        o_ref[...] = (acc * lax.rsqrt(ms + EPS) * rest[0][...]).T


def mix_ffn(xt, ya, yb, sga, sgb, wa_t, wb_t, wo_t, g2, wgu_t, wd_t, layer, final_g=None):
    t = xt.shape[1]
    last = final_g is not None

    def fm(rows):
        return pl.BlockSpec((rows, TM), lambda i: (0, i))

    def res(shape):
        return _of_layer(shape, layer, pipeline_mode=pl.Buffered(1))

    return pl.pallas_call(
        _mix_ffn_kernel,
        grid=(t // TM,),
        in_specs=[fm(D_MODEL), pl.BlockSpec((1, A_WIDTH, TM), lambda i: (i, 0, 0)),
                  pl.BlockSpec((TM // NA_PAIR, B_WIDTH, NA_PAIR), lambda i: (i, 0, 0)), fm(D_MODEL), fm(D_MODEL),
                  res((D_MODEL, A_WIDTH)), res((D_MODEL, B_WIDTH)), res((D_MODEL, D_MODEL)),
                  _resident((D_MODEL, 1)), res((2 * D_FF, D_MODEL)), res((D_MODEL, D_FF))]
        + ([_resident((D_MODEL, 1))] if last else []),
        out_specs=pl.BlockSpec((TM, D_MODEL), lambda i: (i, 0)) if last else fm(D_MODEL),
        out_shape=jax.ShapeDtypeStruct((t, D_MODEL) if last else (D_MODEL, t), f32),
        compiler_params=_params("parallel"),
        name="mix_ffn",
    )(xt, ya, yb, sga, sgb, wa_t, wb_t, wo_t, g2.reshape(D_MODEL, 1), wgu_t, wd_t,
      *([final_g.reshape(D_MODEL, 1)] if last else []))


W_COLS = 256


def _weight_t_kernel(w_ref, o_ref):
    o_ref[0] = w_ref[0].T.astype(bf16)


def weight_t(w):
    depth, n_in, n_out = w.shape
    return pl.pallas_call(
        _weight_t_kernel,
        grid=(depth, n_out // W_COLS),
        in_specs=[pl.BlockSpec((1, n_in, W_COLS), lambda l, j: (l, 0, j))],
        out_specs=pl.BlockSpec((1, W_COLS, n_in), lambda l, j: (l, j, 0)),
        out_shape=jax.ShapeDtypeStruct((depth, n_out, n_in), bf16),
        compiler_params=_params("parallel", "parallel"),
        name="weight_t",
    )(w)


def _rope_tables(seq):
    t = jnp.arange(seq, dtype=jnp.int32)
    row = (t // GRID_W).astype(f32)
    col = (t % GRID_W).astype(f32)
    quarter = HEAD_DIM // 4
    inv = ROPE_THETA ** (-jnp.arange(quarter, dtype=f32) / quarter)
    ang_r = inv[:, None] * row[None, :]
    ang_c = inv[:, None] * col[None, :]
    cos = jnp.concatenate([jnp.cos(ang_r)] * 2 + [jnp.cos(ang_c)] * 2, axis=0)
    sin = jnp.concatenate([-jnp.sin(ang_r), jnp.sin(ang_r), -jnp.sin(ang_c), jnp.sin(ang_c)], axis=0)
    return cos, sin


def _trunk(x, weights, final_g):
    batch, seq, _ = x.shape
    assert TM == TQ and seq % max(QG * TQ, TK) == 0 and seq % (NA_UNROLL * NA_PAIR) == 0 and seq // NA_PAIR >= NA_WIN_BLOCKS
    cos, sin = _rope_tables(seq)
    xt = x.reshape(batch * seq, D_MODEL)
    w = weights
    depth = w["norm1"].shape[0]
    for layer in range(depth):
        outs = in_proj(xt, w["norm1"][layer], w["w_in_t"], layer, w["q_norm"][layer], w["k_norm"][layer],
                       cos, sin, seq, token_major_in=layer == 0)
        if layer == 0:
            xt, *outs = outs
        qa, ka, va, qb, kb, vb, sga, sgb = outs
        ya = global_attn(qa, ka, va, batch, seq)
        yb = nbr_attn(qb, kb, vb, w["na_bias"], layer, batch, seq)
        xt = mix_ffn(xt, ya, yb, sga, sgb, w["w_up_a_t"], w["w_up_b_t"], w["w_out_t"],
                     w["norm2"][layer], w["w_gate_up_t"], w["w_down_t"], layer,
                     final_g if layer == depth - 1 else None)
    return xt.reshape(batch, seq, D_MODEL)


def kernel(x_prompt, x_sample, norm1, w_in, q_norm, k_norm, rpb, w_up_a, w_up_b, w_out, norm2, w_gate_up,
           w_down, final_norm):
    weights = dict(norm1=norm1, q_norm=q_norm, k_norm=k_norm, norm2=norm2, na_bias=na_bias_tiles(rpb),
                   w_in_t=weight_t(w_in), w_up_a_t=weight_t(w_up_a), w_up_b_t=weight_t(w_up_b),
                   w_out_t=weight_t(w_out), w_gate_up_t=weight_t(w_gate_up), w_down_t=weight_t(w_down))
    return (_trunk(x_prompt, weights, final_norm), _trunk(x_sample, weights, final_norm))
```

```python
import numpy as np
import jax
import jax.numpy as jnp
from jax import lax
from jax.experimental import pallas as pl
from jax.experimental.pallas import tpu as pltpu

D_MODEL = 1024
GRID_W = 64
HEAD_DIM = 64
A_Q_HEADS = 8
A_KV_HEADS = 2
A_GROUPS = A_Q_HEADS // A_KV_HEADS
B_HEADS = 8
NA_ROWS = 8
NA_COLS = 16
ROPE_THETA = 10000.0
EPS = 1e-6
D_FF = -(-8 * D_MODEL // (3 * 256)) * 256
A_WIDTH = A_Q_HEADS * HEAD_DIM
A_KV_WIDTH = A_KV_HEADS * HEAD_DIM
B_WIDTH = B_HEADS * HEAD_DIM
QK_SCALE = HEAD_DIM ** -0.5
LOG2E = 1.4426950408889634
BF16_TILE_ROWS = 16
V_ROWS = HEAD_DIM + BF16_TILE_ROWS
NEG = -1e30

R_QA = 0
R_KA = R_QA + A_WIDTH
R_VA = R_KA + A_KV_WIDTH
R_QB = R_VA + A_KV_WIDTH
R_KB = R_QB + B_WIDTH
R_VB = R_KB + B_WIDTH
R_GA = R_VB + B_WIDTH
R_GB = R_GA + D_MODEL
IN_COLS = R_GB + D_MODEL

NA_PAIR = 2 * GRID_W
NA_WIN_BLOCKS = 5
NA_WIN = NA_WIN_BLOCKS * NA_PAIR
NA_CLASSES = 5

VMEM_LIMIT = 56 * 1024 * 1024

TM = 512
SUB = 256
TQ = 512
TK = 1024
CB = 256
QG = 16

f32 = jnp.float32
bf16 = jnp.bfloat16


def _params(*sem):
    return pltpu.CompilerParams(dimension_semantics=sem, vmem_limit_bytes=VMEM_LIMIT)


def _full(shape):
    return pl.BlockSpec(shape, lambda *_: (0,) * len(shape))


def _of_layer(shape, layer, **kw):
    return pl.BlockSpec((None,) + tuple(shape), lambda *_: (layer,) + (0,) * len(shape), **kw)


def _head_norm_rope(y, gain, cos, sin):
    nh = y.shape[0] // HEAD_DIM
    y = y.reshape(nh, HEAD_DIM, y.shape[1])
    ms = jnp.mean(y * y, axis=1, keepdims=True)
    y = y * lax.rsqrt(ms + EPS) * gain
    q = HEAD_DIM // 4
    rot = jnp.concatenate([y[:, q:2 * q], y[:, 0:q], y[:, 3 * q:4 * q], y[:, 2 * q:3 * q]], axis=1)
    y = y * cos + rot * sin
    return y.reshape(nh * HEAD_DIM, y.shape[2])


def _in_proj_kernel(x_ref, g1_ref, w_ref, qg_ref, kg_ref, cos_ref, sin_ref, *outs):
    if len(outs) == 9:
        xt_ref, *outs = outs
        x_all = x_ref[...].T
        xt_ref[...] = x_all
    else:
        x_all = x_ref[...]
    qa_ref, ka_ref, va_ref, qb_ref, kb_ref, vb_ref, sga_ref, sgb_ref = outs
    nsub = x_all.shape[1] // SUB
    hs = []
    for t in range(nsub):
        x = x_all[:, t * SUB:(t + 1) * SUB]
        ms = jnp.mean(x * x, axis=0, keepdims=True)
        hs.append((x * lax.rsqrt(ms + EPS) * g1_ref[...]).astype(bf16))
    for t in range(nsub):
        tc = slice(t * SUB, (t + 1) * SUB)
        h = hs[t]
        cos = cos_ref[:, tc]
        sin = sin_ref[:, tc]

        def proj(r0, r1, h=h):
            return jnp.dot(w_ref[r0:r1, :], h, preferred_element_type=f32)

        qa = _head_norm_rope(proj(R_QA, R_KA), qg_ref[...], cos, sin)
        qa_ref[0, :, tc] = (qa * (QK_SCALE * LOG2E)).astype(bf16)
        ka = _head_norm_rope(proj(R_KA, R_VA), kg_ref[...], cos, sin)
        ka_ref[tc, :] = ka.T.astype(bf16)
        va = proj(R_VA, R_QB).astype(bf16)
        ones = jnp.ones((V_ROWS - HEAD_DIM, SUB), bf16)
        va_ref[0, :, tc] = jnp.concatenate([va[0:HEAD_DIM], ones, va[HEAD_DIM:2 * HEAD_DIM], ones], axis=0)
        qb = (proj(R_QB, R_KB) * (QK_SCALE * LOG2E)).astype(bf16)
        kb_ref[tc, :] = proj(R_KB, R_VB).T.astype(bf16)
        vb = proj(R_VB, R_GA).astype(bf16)
        for c in range(SUB // NA_PAIR):
            pc = slice(c * NA_PAIR, (c + 1) * NA_PAIR)
            qb_ref[t * (SUB // NA_PAIR) + c] = qb[:, pc]
            vb_ref[t * (SUB // NA_PAIR) + c] = vb[:, pc]
        sga_ref[:, tc] = jax.nn.sigmoid(proj(R_GA, R_GB)).astype(bf16)
        sgb_ref[:, tc] = jax.nn.sigmoid(proj(R_GB, IN_COLS)).astype(bf16)


def in_proj(x, g1, w_in_t, layer, qg, kg, cos, sin, seq, token_major_in=False):
    t = x.shape[0] if token_major_in else x.shape[1]
    n_seq_tiles = seq // TM
    tab = pl.BlockSpec((HEAD_DIM, TM), lambda i: (0, i % n_seq_tiles))

    def fm(rows):
        return pl.BlockSpec((rows, TM), lambda i: (0, i))

    def tok(cols):
        return pl.BlockSpec((TM, cols), lambda i: (i, 0))

    pair_tiles = pl.BlockSpec((TM // NA_PAIR, B_WIDTH, NA_PAIR), lambda i: (i, 0, 0))
    out_shapes = (
        jax.ShapeDtypeStruct((t // TM, A_WIDTH, TM), bf16),
        jax.ShapeDtypeStruct((t, A_KV_WIDTH), bf16),
        jax.ShapeDtypeStruct((t // TK, A_KV_HEADS * V_ROWS, TK), bf16),
        jax.ShapeDtypeStruct((t // NA_PAIR, B_WIDTH, NA_PAIR), bf16),
        jax.ShapeDtypeStruct((t, B_WIDTH), bf16),
        jax.ShapeDtypeStruct((t // NA_PAIR, B_WIDTH, NA_PAIR), bf16),
        jax.ShapeDtypeStruct((D_MODEL, t), bf16),
        jax.ShapeDtypeStruct((D_MODEL, t), bf16),
    )
    out_specs = (
        pl.BlockSpec((1, A_WIDTH, TM), lambda i: (i, 0, 0)), tok(A_KV_WIDTH),
        pl.BlockSpec((1, A_KV_HEADS * V_ROWS, TM), lambda i: (i // (TK // TM), 0, i % (TK // TM))),
        pair_tiles, tok(B_WIDTH), pair_tiles, fm(D_MODEL), fm(D_MODEL),
    )
    if token_major_in:
        out_shapes = (jax.ShapeDtypeStruct((D_MODEL, t), f32),) + out_shapes
        out_specs = (fm(D_MODEL),) + out_specs
    return pl.pallas_call(
        _in_proj_kernel,
        grid=(t // TM,),
        in_specs=[tok(D_MODEL) if token_major_in else fm(D_MODEL), _full((D_MODEL, 1)), _of_layer((IN_COLS, D_MODEL), layer),
                  _full((HEAD_DIM, 1)), _full((HEAD_DIM, 1)), tab, tab],
        out_specs=out_specs,
        out_shape=out_shapes,
        compiler_params=_params("parallel"),
        name="in_proj",
    )(x, g1.reshape(D_MODEL, 1), w_in_t, qg.reshape(HEAD_DIM, 1), kg.reshape(HEAD_DIM, 1), cos, sin)


def _global_attn_kernel(q_ref, k_ref, v_ref, o_ref, s_scr):
    n_q, _, tq = q_ref.shape
    n_kv = v_ref.shape[0]
    tk = v_ref.shape[2]
    n = A_GROUPS * tq
    j = pl.program_id(1)
    ncb = n // CB
    cols = [slice(c * CB, (c + 1) * CB) for c in range(ncb)]

    def load_q(qi):
        qt = q_ref[qi]
        q = jnp.concatenate([qt[g * HEAD_DIM:(g + 1) * HEAD_DIM, :] for g in range(A_GROUPS)], axis=1)
        zeros = jnp.zeros_like(q)
        qpad = jnp.concatenate([jnp.where(j == 0, q, zeros), jnp.where(j == 1, q, zeros)], axis=0)
        return [qpad[:, c] for c in cols]

    def produce(i, slot, c, qblk):
        k = k_ref[pl.ds(pl.multiple_of(i * tk, tk), tk), :]
        s = jnp.dot(k, qblk[c], preferred_element_type=f32)
        s_scr[slot, :, cols[c]] = s
        return jnp.max(s, axis=0, keepdims=True)

    def consume(i, slot, c, mt, m, l, acc):
        m_new = jnp.maximum(m, mt)
        alpha = jnp.exp2(m - m_new)
        p = jnp.exp2(s_scr[slot, :, cols[c]] - m_new).astype(bf16)
        pv = jnp.dot(v_ref[i], p, preferred_element_type=f32)
        l = alpha * l + pv[HEAD_DIM:HEAD_DIM + 1, :]
        acc = alpha * acc + pv[0:HEAD_DIM, :]
        return m_new, l, acc

    def half(i_cur, slot_cur, state, qblk, i_next):
        out = []
        for c in range(ncb):
            mt, m, l, acc = state[c]
            mt_next = produce(i_next, 1 - slot_cur, c, qblk)
            out.append((mt_next,) + consume(i_cur, slot_cur, c, mt, m, l, acc))
        return out

    def q_tile(qi, mts):
        qblk = load_q(qi)
        state = [(mts[c], jnp.full((1, CB), NEG, f32), jnp.zeros((1, CB), f32), jnp.zeros((HEAD_DIM, CB), f32))
                 for c in range(ncb)]

        def body(it, state):
            i = 2 * it
            return half(i + 1, 1, half(i, 0, state, qblk, i + 1), qblk, i + 2)

        state = lax.fori_loop(0, n_kv // 2 - 1, body, state)
        state = half(n_kv - 2, 0, state, qblk, n_kv - 1)
        state = half(n_kv - 1, 1, state, load_q(jnp.minimum(qi + 1, n_q - 1)), 0)
        out = jnp.concatenate([(acc / l).astype(bf16) for _, _, l, acc in state], axis=1)
        o_ref[qi] = jnp.concatenate([out[:, g * tq:(g + 1) * tq] for g in range(A_GROUPS)], axis=0)
        return [st[0] for st in state]

    qblk0 = load_q(0)
    lax.fori_loop(0, n_q, q_tile, [produce(0, 0, c, qblk0) for c in range(ncb)])


def global_attn(qa, ka, va, batch, seq):
    n_g = seq // (QG * TQ)
    n_kv = seq // TK
    qo_spec = pl.BlockSpec((QG, A_GROUPS * HEAD_DIM, TQ), lambda b, j, g: (b * n_g + g, j, 0))
    return pl.pallas_call(
        _global_attn_kernel,
        grid=(batch, A_KV_HEADS, n_g),
        in_specs=[qo_spec,
                  pl.BlockSpec((seq, A_KV_WIDTH), lambda b, j, g: (b, 0)),
                  pl.BlockSpec((n_kv, V_ROWS, TK), lambda b, j, g: (b, j, 0))],
        out_specs=qo_spec,
        out_shape=jax.ShapeDtypeStruct(qa.shape, bf16),
        scratch_shapes=[pltpu.VMEM((2, TK, A_GROUPS * TQ), f32)],
        compiler_params=_params("parallel", "parallel", "parallel"),
        name="global_attn",
    )(qa, ka, va)


def _na_bias_index():
    x = np.arange(GRID_W)[:, None]
    c = np.arange(GRID_W)[None, :]
    cs = np.clip(c - NA_COLS // 2, 0, GRID_W - NA_COLS)
    col_ok = (x >= cs) & (x < cs + NA_COLS)
    dc = np.clip(x - c, -(NA_COLS - 1), NA_COLS - 1) + (NA_COLS - 1)
    onehot = (dc[None] == np.arange(2 * NA_COLS - 1)[:, None, None]).astype(np.float32)
    cls = [(0, (0, 0)), (2, (0, 0)), (4, (0, 1)), (6, (2, 2)), (8, (2, 2))]
    i = np.arange(NA_WIN_BLOCKS * 2)[:, None]
    u = np.arange(2)[None, :]
    rel, row_ok = [], []
    for delta, rs_off in cls:
        off = np.asarray(rs_off)[None, :]
        rel.append(np.clip(i - delta - u + (NA_ROWS - 1), 0, 2 * NA_ROWS - 2))
        row_ok.append((i - off >= 0) & (i - off < NA_ROWS))
    return onehot, col_ok, np.stack(rel).astype(np.int32), np.stack(row_ok)


def na_bias_tiles(rpb):
    onehot, col_ok, rel, row_ok = _na_bias_index()
    depth = rpb.shape[0]
    t1 = jnp.einsum("lhad,dxc->lhaxc", rpb.astype(f32), jnp.asarray(onehot), precision=lax.Precision.HIGHEST)
    parts = []
    for e in range(2):
        for u in range(2):
            g = jnp.take(t1[:, e::2], jnp.asarray(rel[:, :, u].reshape(-1)), axis=2)
            ok = (row_ok[:, :, u, None, None] & col_ok[None, None]).reshape(-1, GRID_W, GRID_W)
            parts.append(jnp.where(jnp.asarray(ok)[None, None], g, NEG))
    g = jnp.concatenate(parts, axis=-1)
    return g.reshape(depth, B_HEADS // 2, NA_CLASSES, NA_WIN, 2 * NA_PAIR) * LOG2E


NA_UNROLL = 16


def _nbr_attn_kernel(q_ref, k_ref, v_ref, b_ref, o_ref, s_scr, p_scr):
    n_pairs = q_ref.shape[0]
    zeros = jnp.zeros((HEAD_DIM, NA_PAIR), bf16)

    def win(p):
        return jnp.clip(p - 2, 0, n_pairs - NA_WIN_BLOCKS)

    def cls(p):
        return jnp.where(p < 2, p, jnp.where(p >= n_pairs - 2, p - (n_pairs - 2) + 3, 2))

    def scores(p, slot):
        q2 = q_ref[p]
        qbd = jnp.concatenate([jnp.concatenate([q2[0:HEAD_DIM], zeros], axis=1),
                               jnp.concatenate([zeros, q2[HEAD_DIM:]], axis=1)], axis=0)
        w, c = win(p), cls(p)
        mx = None
        for r in range(NA_WIN_BLOCKS):
            rr = slice(r * NA_PAIR, (r + 1) * NA_PAIR)
            k = k_ref[pl.ds(pl.multiple_of((w + r) * NA_PAIR, NA_PAIR), NA_PAIR), :]
            x = jnp.dot(k, qbd, preferred_element_type=f32) + b_ref[0, c, rr, :]
            s_scr[slot, rr, :] = x
            xm = jnp.max(x, axis=0, keepdims=True)
            mx = xm if mx is None else jnp.maximum(mx, xm)
        return mx

    def probs(slot, m):
        p = jnp.exp2(s_scr[slot] - m)
        p_scr[slot] = p.astype(bf16)
        return jnp.sum(p, axis=0, keepdims=True)

    def weighted(p, slot, l):
        w = win(p)
        vwin = jnp.concatenate([v_ref[w + r] for r in range(NA_WIN_BLOCKS)], axis=1)
        o = jnp.dot(vwin, p_scr[slot], preferred_element_type=f32) / l
        o_ref[p] = jnp.concatenate([o[0:HEAD_DIM, 0:NA_PAIR], o[HEAD_DIM:, NA_PAIR:]], axis=0).astype(bf16)

    def step(p, slot, carry):
        m_next, l_cur = carry
        m_new = scores(jnp.minimum(p + 2, n_pairs - 1), slot)
        l_next = probs(1 - slot, m_next)
        weighted(p, slot, l_cur)
        return m_new, l_next

    m0 = scores(0, 0)
    m1 = scores(1, 1)
    l0 = probs(0, m0)

    def body(it, carry):
        for r in range(NA_UNROLL):
            carry = step(NA_UNROLL * it + r, r % 2, carry)
        return carry

    lax.fori_loop(0, n_pairs // NA_UNROLL, body, (m1, l0))


def nbr_attn(qb, kb, vb, bias, layer, batch, seq):
    n_pairs = seq // NA_PAIR
    tile = pl.BlockSpec((n_pairs, 2 * HEAD_DIM, NA_PAIR), lambda b, hp: (b, hp, 0))
    return pl.pallas_call(
        _nbr_attn_kernel,
        grid=(batch, B_HEADS // 2),
        in_specs=[tile, pl.BlockSpec((seq, 2 * HEAD_DIM), lambda b, hp: (b, hp)), tile,
                  pl.BlockSpec((None, 1, NA_CLASSES, NA_WIN, 2 * NA_PAIR), lambda b, hp: (layer, hp, 0, 0, 0))],
        out_specs=tile,
        out_shape=jax.ShapeDtypeStruct(qb.shape, bf16),
        scratch_shapes=[pltpu.VMEM((2, NA_WIN, 2 * NA_PAIR), f32), pltpu.VMEM((2, NA_WIN, 2 * NA_PAIR), bf16)],
        compiler_params=_params("parallel", "parallel"),
        name="nbr_attn",
    )(qb, kb, vb, bias)


def _resident(shape):
    return pl.BlockSpec(shape, lambda *_: (0,) * len(shape), pipeline_mode=pl.Buffered(1))


FF_CHUNKS = ((0, 1024), (1024, 2048), (2048, D_FF))


def _mix_ffn_kernel(x_ref, ya_ref, yb_ref, sga_ref, sgb_ref, wa_ref, wb_ref, wo_ref, g2_ref, wgu_ref, wd_ref, *rest):
    o_ref = rest[-1]
    ua = jnp.dot(wa_ref[...], ya_ref[0], preferred_element_type=f32)
    yb = jnp.concatenate([yb_ref[c] for c in range(yb_ref.shape[0])], axis=1)
    ub = jnp.dot(wb_ref[...], yb, preferred_element_type=f32)
    mix = sga_ref[...].astype(f32) * ua + sgb_ref[...].astype(f32) * ub
    x = x_ref[...] + jnp.dot(wo_ref[...], mix.astype(bf16), preferred_element_type=f32)
    ms = jnp.mean(x * x, axis=0, keepdims=True)
    h = (x * lax.rsqrt(ms + EPS) * g2_ref[...]).astype(bf16)
    acc = x
    for c0, c1 in FF_CHUNKS:
        gate = jnp.dot(wgu_ref[c0:c1, :], h, preferred_element_type=f32)
        up = jnp.dot(wgu_ref[D_FF + c0:D_FF + c1, :], h, preferred_element_type=f32)
        act = (jax.nn.silu(gate) * up).astype(bf16)
        acc = acc + jnp.dot(wd_ref[:, c0:c1], act, preferred_element_type=f32)
    if len(rest) == 1:
        o_ref[...] = acc
    else:
        ms = jnp.mean(acc * acc, axis=0, keepdims=True)
        o_ref[...] = (acc * lax.rsqrt(ms + EPS) * rest[0][...]).T


def mix_ffn(xt, ya, yb, sga, sgb, wa_t, wb_t, wo_t, g2, wgu_t, wd_t, layer, final_g=None):
    t = xt.shape[1]
    last = final_g is not None

    def fm(rows):
        return pl.BlockSpec((rows, TM), lambda i: (0, i))

    def res(shape):
        return _of_layer(shape, layer, pipeline_mode=pl.Buffered(1))

    return pl.pallas_call(
        _mix_ffn_kernel,
        grid=(t // TM,),
        in_specs=[fm(D_MODEL), pl.BlockSpec((1, A_WIDTH, TM), lambda i: (i, 0, 0)),
                  pl.BlockSpec((TM // NA_PAIR, B_WIDTH, NA_PAIR), lambda i: (i, 0, 0)), fm(D_MODEL), fm(D_MODEL),
                  res((D_MODEL, A_WIDTH)), res((D_MODEL, B_WIDTH)), res((D_MODEL, D_MODEL)),
                  _resident((D_MODEL, 1)), res((2 * D_FF, D_MODEL)), res((D_MODEL, D_FF))]
        + ([_resident((D_MODEL, 1))] if last else []),
        out_specs=pl.BlockSpec((TM, D_MODEL), lambda i: (i, 0)) if last else fm(D_MODEL),
        out_shape=jax.ShapeDtypeStruct((t, D_MODEL) if last else (D_MODEL, t), f32),
        compiler_params=_params("parallel"),
        name="mix_ffn",
    )(xt, ya, yb, sga, sgb, wa_t, wb_t, wo_t, g2.reshape(D_MODEL, 1), wgu_t, wd_t,
      *([final_g.reshape(D_MODEL, 1)] if last else []))


W_COLS = 256


def _weight_t_kernel(w_ref, o_ref):
    o_ref[0] = w_ref[0].T.astype(bf16)


def weight_t(w):
    depth, n_in, n_out = w.shape
    return pl.pallas_call(
        _weight_t_kernel,
        grid=(depth, n_out // W_COLS),
        in_specs=[pl.BlockSpec((1, n_in, W_COLS), lambda l, j: (l, 0, j))],
        out_specs=pl.BlockSpec((1, W_COLS, n_in), lambda l, j: (l, j, 0)),
        out_shape=jax.ShapeDtypeStruct((depth, n_out, n_in), bf16),
        compiler_params=_params("parallel", "parallel"),
        name="weight_t",
    )(w)


def _rope_tables(seq):
    t = jnp.arange(seq, dtype=jnp.int32)
    row = (t // GRID_W).astype(f32)
    col = (t % GRID_W).astype(f32)
    quarter = HEAD_DIM // 4
    inv = ROPE_THETA ** (-jnp.arange(quarter, dtype=f32) / quarter)
    ang_r = inv[:, None] * row[None, :]
    ang_c = inv[:, None] * col[None, :]
    cos = jnp.concatenate([jnp.cos(ang_r)] * 2 + [jnp.cos(ang_c)] * 2, axis=0)
    sin = jnp.concatenate([-jnp.sin(ang_r), jnp.sin(ang_r), -jnp.sin(ang_c), jnp.sin(ang_c)], axis=0)
    return cos, sin


def _trunk(x, weights, final_g):
    batch, seq, _ = x.shape
    assert TM == TQ and seq % max(QG * TQ, TK) == 0 and seq % (NA_UNROLL * NA_PAIR) == 0 and seq // NA_PAIR >= NA_WIN_BLOCKS
    cos, sin = _rope_tables(seq)
    xt = x.reshape(batch * seq, D_MODEL)
    w = weights
    depth = w["norm1"].shape[0]
    for layer in range(depth):
        outs = in_proj(xt, w["norm1"][layer], w["w_in_t"], layer, w["q_norm"][layer], w["k_norm"][layer],
                       cos, sin, seq, token_major_in=layer == 0)
        if layer == 0:
            xt, *outs = outs
        qa, ka, va, qb, kb, vb, sga, sgb = outs
        ya = global_attn(qa, ka, va, batch, seq)
        yb = nbr_attn(qb, kb, vb, w["na_bias"], layer, batch, seq)
        xt = mix_ffn(xt, ya, yb, sga, sgb, w["w_up_a_t"], w["w_up_b_t"], w["w_out_t"],
                     w["norm2"][layer], w["w_gate_up_t"], w["w_down_t"], layer,
                     final_g if layer == depth - 1 else None)
    return xt.reshape(batch, seq, D_MODEL)


def kernel(x_prompt, x_sample, norm1, w_in, q_norm, k_norm, rpb, w_up_a, w_up_b, w_out, norm2, w_gate_up,
           w_down, final_norm):
    weights = dict(norm1=norm1, q_norm=q_norm, k_norm=k_norm, norm2=norm2, na_bias=na_bias_tiles(rpb),
                   w_in_t=weight_t(w_in), w_up_a_t=weight_t(w_up_a), w_up_b_t=weight_t(w_up_b),
                   w_out_t=weight_t(w_out), w_gate_up_t=weight_t(w_gate_up), w_down_t=weight_t(w_down))
    return (_trunk(x_prompt, weights, final_norm), _trunk(x_sample, weights, final_norm))
```
